```python
import math
import jax, jax.numpy as jnp
from jax import lax
import numpy as np

D_MODEL = 2048
BATCH = 2
SEQ = 16384
DEPTH = 1
DEC_BATCH = 32
DEC_SEQ = 32
PAST_LEN = 1024

CHUNK = 64
Q_BLOCK = 128
EPS = 1e-6
A_HEADS = 8
A_DIM = 64
A_VDIM = 2 * A_DIM
R_HEADS = 8
R_KDIM = 128
R_VDIM = 256
PEER_HEADS = 8
N_KEYS = 128
N_EXPERTS = N_KEYS * N_KEYS
PEER_QDIM = 256
PEER_HALF = PEER_QDIM // 2
PEER_TOPK = 16
PEER_BLOCK = 64
A_QK = A_HEADS * 2 * A_DIM
A_V = A_HEADS * A_VDIM
R_QK = R_HEADS * R_KDIM
R_V = R_HEADS * R_VDIM
IN_COLS = 2 * A_QK + A_V + 2 * R_QK + 2 * R_V + 2 * D_MODEL

kernel_name = "diffattn_retnet_peer_streaming_step"


def rmsnorm(x, g):
    xf = x.astype(jnp.float32)
    xf = xf * lax.rsqrt(jnp.mean(xf * xf, axis=-1, keepdims=True) + EPS)
    return xf.astype(x.dtype) * g


def rotate(x, pos):
    half = R_KDIM // 2
    freqs = 10000.0 ** (-jnp.arange(half, dtype=jnp.float32) / half)
    ang = pos.astype(jnp.float32)[:, None] * freqs[None, :]
    cos = jnp.cos(ang)[None, :, None, :]
    sin = jnp.sin(ang)[None, :, None, :]
    x1, x2 = x[..., :half], x[..., half:]
    return jnp.concatenate([x1 * cos - x2 * sin, x1 * sin + x2 * cos], axis=-1).astype(x.dtype)


def diff_attend(q, k, v, lam, mask):
    s = jnp.einsum('bqhcd,bkhcd->bchqk', q, k).astype(jnp.float32) * (A_DIM ** -0.5)
    if mask is not None:
        s = jnp.where(mask, s, -1e30)
    p = jax.nn.softmax(s, axis=-1)
    a = p[:, 0] - lam * p[:, 1]
    return jnp.einsum('bhqk,bkhe->bqhe', a.astype(v.dtype), v)


def diff_attn_prompt(q, k, v, lam):
    B, S = q.shape[:2]
    nqb = S // Q_BLOCK
    qb = q.reshape(B, nqb, Q_BLOCK, A_HEADS, 2, A_DIM).swapaxes(0, 1)
    k_chunk = jnp.arange(S) // CHUNK

    def body(args):
        q_blk, i = args
        q_chunk = (i * Q_BLOCK + jnp.arange(Q_BLOCK)) // CHUNK
        mask = k_chunk[None, :] <= q_chunk[:, None]
        return diff_attend(q_blk, k, v, lam, mask)

    o = lax.map(body, (qb, jnp.arange(nqb)))
    return o.swapaxes(0, 1).reshape(B, S, A_HEADS, A_VDIM)


def retention_chunk(q, k, v, state, log_gamma):
    L = q.shape[1]
    i = jnp.arange(L, dtype=jnp.float32)
    d_intra = jnp.exp(log_gamma[:, None, None] * jnp.abs(i[:, None] - i[None, :])[None])
    s = jnp.einsum('bihd,bjhd->bhij', q, k) * d_intra
    intra = jnp.einsum('bhij,bjhe->bihe', s, v)
    d_in = jnp.exp(log_gamma[None, :] * (i[:, None] + 1.0))[None, :, :, None]
    cross = jnp.einsum('bihd,bhde->bihe', q, state) * d_in
    d_out = jnp.exp(log_gamma[None, :] * (L - 1.0 - i[:, None]))[None, :, :, None]
    new_state = (jnp.exp(log_gamma * L)[None, :, None, None] * state
                 + jnp.einsum('bjhd,bjhe->bhde', k * d_out, v))
    return (intra + cross).astype(v.dtype), new_state.astype(state.dtype)


def retention_prompt(q, k, v, log_gamma):
    B, S = q.shape[:2]
    nc = S // CHUNK

    def to_chunks(t):
        return t.reshape(B, nc, CHUNK, *t.shape[2:]).swapaxes(0, 1)

    s0 = jnp.zeros((B, R_HEADS, R_KDIM, R_VDIM), v.dtype)

    def step(st, xs):
        qc, kc, vc = xs
        o, st = retention_chunk(qc, kc, vc, st, log_gamma)
        return st, o

    st, o = lax.scan(step, s0, (to_chunks(q), to_chunks(k), to_chunks(v)))
    return o.swapaxes(0, 1).reshape(B, S, R_HEADS, R_VDIM), st


def peer_ffn(h, wq, keys, u_tab, v_tab):
    lead = h.shape[:-1]
    t = h.reshape(-1, D_MODEL)
    T = t.shape[0]
    nb = -(-T // PEER_BLOCK)
    t = jnp.pad(t, ((0, nb * PEER_BLOCK - T), (0, 0))).reshape(nb, PEER_BLOCK, D_MODEL)

    def block(tb):
        q = (tb @ wq).reshape(PEER_BLOCK, PEER_HEADS, 2, PEER_HALF)
        s = jnp.einsum('thcd,hcnd->thcn', q, keys).astype(jnp.float32)
        sv, si = lax.top_k(s, PEER_TOPK)
        cand = sv[:, :, 0, :, None] + sv[:, :, 1, None, :]
        cs, ci = lax.top_k(cand.reshape(PEER_BLOCK, PEER_HEADS, PEER_TOPK * PEER_TOPK), PEER_TOPK)
        i1 = jnp.take_along_axis(si[:, :, 0], ci // PEER_TOPK, axis=-1)
        i2 = jnp.take_along_axis(si[:, :, 1], ci % PEER_TOPK, axis=-1)
        expert = i1 * N_KEYS + i2
        g = jax.nn.softmax(cs, axis=-1)
        u = jnp.take(u_tab, expert, axis=0)
        act = jax.nn.gelu(jnp.einsum('td,thkd->thk', tb, u).astype(jnp.float32), approximate=False) * g
        vv = jnp.take(v_tab, expert, axis=0)
        return jnp.einsum('thk,thkd->td', act.astype(tb.dtype), vv)

    out = lax.map(block, t).reshape(-1, D_MODEL)[:T]
    return out.reshape(*lead, D_MODEL)


def layer(x, pos, k_past, v_past, ret_state, lam_init, norm_mix, w_in, lambda_q1, lambda_k1,
          lambda_q2, lambda_k2, attn_subln, ret_gn, w_branch_a, w_branch_r, w_out, norm_ffn,
          peer_wq, peer_keys, peer_u, peer_v):
    B, L = x.shape[:2]
    h = rmsnorm(x, norm_mix)
    sizes = (A_QK, A_QK, A_V, R_QK, R_QK, R_V, R_V, D_MODEL, D_MODEL)
    offs = [int(o) for o in np.cumsum(sizes)[:-1]]
    qa, ka, va, qr, kr, vr, g_swish, g_a, g_r = jnp.split(h @ w_in, offs, axis=-1)

    qa = qa.reshape(B, L, A_HEADS, 2, A_DIM)
    k_rows = ka.reshape(B, L, A_HEADS, 2 * A_DIM)
    v_rows = va.reshape(B, L, A_HEADS, A_VDIM)
    lam = (jnp.exp(jnp.sum(lambda_q1.astype(jnp.float32) * lambda_k1.astype(jnp.float32)))
           - jnp.exp(jnp.sum(lambda_q2.astype(jnp.float32) * lambda_k2.astype(jnp.float32)))
           + lam_init)
    if k_past is None:
        o_a = diff_attn_prompt(qa, k_rows.reshape(B, L, A_HEADS, 2, A_DIM), v_rows, lam)
    else:
        k_all = jnp.concatenate([k_past, k_rows], axis=1)
        k_all = k_all.reshape(B, k_all.shape[1], A_HEADS, 2, A_DIM)
        v_all = jnp.concatenate([v_past, v_rows], axis=1)
        o_a = diff_attend(qa, k_all, v_all, lam, None)
    o_a = (rmsnorm(o_a, attn_subln) * (1.0 - lam_init)).reshape(B, L, A_V)

    log_gamma = jnp.log(1.0 - 2.0 ** (-5.0 - jnp.arange(R_HEADS, dtype=jnp.float32)))
    qr = rotate(qr.reshape(B, L, R_HEADS, R_KDIM), pos)
    kr = rotate(kr.reshape(B, L, R_HEADS, R_KDIM) * (R_KDIM ** -0.5), pos)
    vr = vr.reshape(B, L, R_HEADS, R_VDIM)
    if ret_state is None:
        o_r, new_state = retention_prompt(qr, kr, vr, log_gamma)
    else:
        o_r, new_state = retention_chunk(qr, kr, vr, ret_state, log_gamma)
    o_r = jax.nn.silu(g_swish) * rmsnorm(o_r, ret_gn).reshape(B, L, R_V)

    merged = jax.nn.sigmoid(g_a) * (o_a @ w_branch_a) + jax.nn.sigmoid(g_r) * (o_r @ w_branch_r)
    x = x + merged @ w_out
    x = x + peer_ffn(rmsnorm(x, norm_ffn), peer_wq, peer_keys, peer_u, peer_v)
    return x, k_rows, v_rows, new_state


def setup_inputs(seed: int = 0) -> dict:
    key = jax.random.key(seed)
    ks = jax.random.split(key, 24)

    def nrm(k, shape, scale):
        return jax.random.normal(k, shape, jnp.float32) * scale

    return {
        "x_prompt": nrm(ks[0], (BATCH, SEQ, D_MODEL), 1.0),
        "x_sample": nrm(ks[1], (DEC_BATCH, DEC_SEQ, D_MODEL), 1.0),
        "cache_k": nrm(ks[2], (DEPTH, DEC_BATCH, PAST_LEN, A_HEADS, 2 * A_DIM), 1.0),
        "cache_v": nrm(ks[3], (DEPTH, DEC_BATCH, PAST_LEN, A_HEADS, A_VDIM), 1.0),
        "state_ret": nrm(ks[4], (DEPTH, DEC_BATCH, R_HEADS, R_KDIM, R_VDIM), 0.5),
        "norm_mix": 1.0 + nrm(ks[5], (DEPTH, D_MODEL), 0.02),
        "w_in": nrm(ks[6], (DEPTH, D_MODEL, IN_COLS), D_MODEL ** -0.5),
        "lambda_q1": nrm(ks[7], (DEPTH, A_DIM), 0.1),
        "lambda_k1": nrm(ks[8], (DEPTH, A_DIM), 0.1),
        "lambda_q2": nrm(ks[9], (DEPTH, A_DIM), 0.1),
        "lambda_k2": nrm(ks[10], (DEPTH, A_DIM), 0.1),
        "attn_subln": 1.0 + nrm(ks[11], (DEPTH, A_VDIM), 0.02),
        "ret_gn": 1.0 + nrm(ks[12], (DEPTH, R_HEADS, R_VDIM), 0.02),
        "w_branch_a": nrm(ks[13], (DEPTH, A_V, D_MODEL), A_V ** -0.5),
        "w_branch_r": nrm(ks[14], (DEPTH, R_V, D_MODEL), R_V ** -0.5),
        "w_out": nrm(ks[15], (DEPTH, D_MODEL, D_MODEL), D_MODEL ** -0.5),
        "norm_ffn": 1.0 + nrm(ks[16], (DEPTH, D_MODEL), 0.02),
        "peer_wq": nrm(ks[17], (DEPTH, D_MODEL, PEER_HEADS * PEER_QDIM), D_MODEL ** -0.5),
        "peer_keys": nrm(ks[18], (DEPTH, PEER_HEADS, 2, N_KEYS, PEER_HALF), PEER_HALF ** -0.5),
        "peer_u": nrm(ks[19], (DEPTH, N_EXPERTS, D_MODEL), D_MODEL ** -0.5),
        "peer_v": nrm(ks[20], (DEPTH, N_EXPERTS, D_MODEL), PEER_HEADS ** -0.5),
        "norm_final": 1.0 + nrm(ks[21], (D_MODEL,), 0.02),
    }


def reference(x_prompt, x_sample, cache_k, cache_v, state_ret, norm_mix, w_in, lambda_q1,
              lambda_k1, lambda_q2, lambda_k2, attn_subln, ret_gn, w_branch_a, w_branch_r,
              w_out, norm_ffn, peer_wq, peer_keys, peer_u, peer_v, norm_final):
    pos_p = jnp.arange(x_prompt.shape[1])
    pos_s = cache_k.shape[2] + jnp.arange(x_sample.shape[1])
    xp, xs = x_prompt, x_sample
    kp_l, vp_l, sp_l, ks_l, vs_l, ss_l = [], [], [], [], [], []
    for l in range(DEPTH):
        lam_init = 0.8 - 0.6 * math.exp(-0.3 * l)
        w = (norm_mix[l], w_in[l], lambda_q1[l], lambda_k1[l], lambda_q2[l], lambda_k2[l],
             attn_subln[l], ret_gn[l], w_branch_a[l], w_branch_r[l], w_out[l], norm_ffn[l],
             peer_wq[l], peer_keys[l], peer_u[l], peer_v[l])
        xp, kp, vp, sp = layer(xp, pos_p, None, None, None, lam_init, *w)
        xs, kn, vn, sn = layer(xs, pos_s, cache_k[l], cache_v[l], state_ret[l], lam_init, *w)
        kp_l.append(kp); vp_l.append(vp); sp_l.append(sp)
        ks_l.append(kn); vs_l.append(vn); ss_l.append(sn)
    y_prompt = rmsnorm(xp, norm_final)
    y_sample = rmsnorm(xs, norm_final)
    new_k_prompt = jnp.stack(kp_l)
    new_v_prompt = jnp.stack(vp_l)
    new_state_prompt = jnp.stack(sp_l)
    new_k_sample = jnp.stack(ks_l)
    new_v_sample = jnp.stack(vs_l)
    new_state_sample = jnp.stack(ss_l)
    return (y_prompt, y_sample, new_k_prompt, new_v_prompt, new_state_prompt, new_k_sample, new_v_sample, new_state_sample)
```

```python
import functools
import math

import jax
import jax.numpy as jnp
from jax import lax
from jax.experimental import pallas as pl
from jax.experimental.pallas import tpu as pltpu

D_MODEL = 2048
CHUNK = 64
EPS = 1e-6
A_HEADS = 8
A_DIM = 64
A_VDIM = 2 * A_DIM
R_HEADS = 8
R_KDIM = 128
R_VDIM = 256
PEER_HEADS = 8
N_KEYS = 128
PEER_HALF = 128
PEER_TOPK = 16
A_QK = A_HEADS * 2 * A_DIM
A_V = A_HEADS * A_VDIM
R_QK = R_HEADS * R_KDIM
R_V = R_HEADS * R_VDIM
LAM_INIT = 0.8 - 0.6 * math.exp(-0.3 * 0)
ROPE_BASE = 10000.0

LANES = 128
MIB = 1024 * 1024

F32 = jnp.float32
BF16 = jnp.bfloat16


def _dot(a, b):
    return jnp.dot(a, b, preferred_element_type=F32)


def _dot_nt(a, b):
    return lax.dot_general(a, b, (((1,), (1,)), ((), ())), preferred_element_type=F32)


def _dot_tn(a, b):
    return lax.dot_general(a, b, (((0,), (0,)), ((), ())), preferred_element_type=F32)


def _sigmoid(x):
    return 1.0 / (1.0 + jnp.exp(-x))


def _params(semantics, vmem_mib):
    return pltpu.CompilerParams(dimension_semantics=semantics, vmem_limit_bytes=int(vmem_mib * MIB))


def _pick(n, pref):
    return pref if n % pref == 0 else n


def _resident(shape):
    nd = len(shape)
    return pl.BlockSpec(shape, lambda *_: (0,) * nd, pipeline_mode=pl.Buffered(1))


def _rmsnorm_kernel(x_ref, g_ref, o_ref):
    x = x_ref[...]
    r = lax.rsqrt(jnp.mean(x * x, axis=-1, keepdims=True) + EPS)
    o_ref[...] = ((x * r) * g_ref[...]).astype(o_ref.dtype)


def _rmsnorm_bf16(x, g):
    t, d = x.shape
    tm = _pick(t, 512)
    return pl.pallas_call(
        _rmsnorm_kernel,
        out_shape=jax.ShapeDtypeStruct((t, d), BF16),
        grid=(t // tm,),
        in_specs=[pl.BlockSpec((tm, d), lambda i: (i, 0)), pl.BlockSpec((1, d), lambda i: (0, 0))],
        out_specs=pl.BlockSpec((tm, d), lambda i: (i, 0)),
        compiler_params=_params(("parallel",), 32),
        name="rmsnorm_mix",
    )(x, g.reshape(1, d))


def _proj_attn_kernel(h_ref, w_ref, q_ref, kf_ref, vf_ref, kb_ref, vb_ref):
    h = h_ref[...]
    q = _dot(h, w_ref[:, 0:A_QK])
    q_ref[...] = (q * (A_DIM ** -0.5)).astype(BF16)
    k = _dot(h, w_ref[:, A_QK:2 * A_QK])
    kf_ref[...] = k
    kb_ref[...] = k.astype(BF16)
    v = _dot(h, w_ref[:, 2 * A_QK:2 * A_QK + A_V])
    vf_ref[...] = v
    vb_ref[...] = v.astype(BF16)


def _proj_attn(h, w):
    t = h.shape[0]
    tm = _pick(t, 512)
    row = lambda i: (i, 0)
    n = A_QK
    return pl.pallas_call(
        _proj_attn_kernel,
        out_shape=(
            jax.ShapeDtypeStruct((t, n), BF16),
            jax.ShapeDtypeStruct((t, n), F32),
            jax.ShapeDtypeStruct((t, n), F32),
            jax.ShapeDtypeStruct((t, n), BF16),
            jax.ShapeDtypeStruct((t, n), BF16),
        ),
        grid=(t // tm,),
        in_specs=[pl.BlockSpec((tm, D_MODEL), row), _resident(w.shape)],
        out_specs=tuple(pl.BlockSpec((tm, n), row) for _ in range(5)),
        compiler_params=_params(("parallel",), 48),
        name="proj_attn",
    )(h, w)


def _proj_ret_kernel(h_ref, w_ref, q_ref, k_ref, v_ref, *, tm, pos_period, pos_offset):
    i = pl.program_id(0)
    h = h_ref[...]
    half = R_KDIM // 2
    row = lax.broadcasted_iota(jnp.int32, (tm, R_KDIM), 0) + i * tm
    pos = (lax.rem(row, pos_period) + pos_offset).astype(F32)
    lane = lax.broadcasted_iota(jnp.int32, (tm, R_KDIM), 1)
    fidx = lax.rem(lane, half).astype(F32)
    freq = jnp.exp(fidx * (-math.log(ROPE_BASE) / half))
    ang = pos * freq
    cos = jnp.cos(ang)
    sin = jnp.sin(ang)
    sin_signed = jnp.where(lane < half, -sin, sin)

    def rotate_into(y, o_ref):
        for hh in range(R_HEADS):
            sl = slice(hh * R_KDIM, (hh + 1) * R_KDIM)
            yh = y[:, sl]
            o_ref[:, sl] = (yh * cos + pltpu.roll(yh, half, 1) * sin_signed).astype(BF16)

    rotate_into(_dot(h, w_ref[:, 0:R_QK]), q_ref)
    rotate_into(_dot(h, w_ref[:, R_QK:2 * R_QK]) * (R_KDIM ** -0.5), k_ref)
    v_ref[...] = _dot(h, w_ref[:, 2 * R_QK:2 * R_QK + R_V]).astype(BF16)


def _proj_ret(h, w, pos_period, pos_offset):
    t = h.shape[0]
    tm = _pick(t, 512)
    if pos_period < tm:
        assert tm % pos_period == 0
    else:
        assert pos_period % tm == 0
    row = lambda i: (i, 0)
    kern = functools.partial(_proj_ret_kernel, tm=tm, pos_period=pos_period, pos_offset=pos_offset)
    return pl.pallas_call(
        kern,
        out_shape=(
            jax.ShapeDtypeStruct((t, R_QK), BF16),
            jax.ShapeDtypeStruct((t, R_QK), BF16),
            jax.ShapeDtypeStruct((t, R_V), BF16),
        ),
        grid=(t // tm,),
        in_specs=[pl.BlockSpec((tm, D_MODEL), row), _resident(w.shape)],
        out_specs=(pl.BlockSpec((tm, R_QK), row), pl.BlockSpec((tm, R_QK), row), pl.BlockSpec((tm, R_V), row)),
        compiler_params=_params(("parallel",), 48),
        name="proj_ret",
    )(h, w)


def _lam_value(lam_ref):
    p = lam_ref[...]
    a = jnp.sum(p[0:1] * p[1:2], axis=-1, keepdims=True)
    b = jnp.sum(p[2:3] * p[3:4], axis=-1, keepdims=True)
    return jnp.exp(a) - jnp.exp(b) + LAM_INIT


def _split_maps(q):
    lane = lax.broadcasted_iota(jnp.int32, q.shape, 1)
    zero = jnp.zeros_like(q)
    return jnp.where(lane < A_DIM, q, zero), jnp.where(lane >= A_DIM, q, zero)


def _subln(o, subln_ref):
    r = lax.rsqrt(jnp.mean(o * o, axis=-1, keepdims=True) + EPS)
    return (((o * r) * subln_ref[...]) * (1.0 - LAM_INIT)).astype(BF16)


def _attn_prompt_kernel(lam_ref, subln_ref, q_ref, k_ref, v_ref, o_ref, *, tq):
    qi = pl.program_id(2)
    lam = _lam_value(lam_ref)
    q1, q2 = _split_maps(q_ref[0])

    def update(s, v, m, l, acc):
        m_new = jnp.maximum(m, jnp.max(s, axis=-1, keepdims=True))
        p = jnp.exp(s - m_new)
        alpha = jnp.exp(m - m_new)
        l = alpha * l + jnp.sum(p, axis=-1, keepdims=True)
        acc = alpha * acc + _dot(p.astype(BF16), v)
        return m_new, l, acc

    def step(j, carry, masked):
        m1, l1, a1, m2, l2, a2 = carry
        start = pl.multiple_of(j * tq, tq)
        k = k_ref[0, pl.ds(start, tq), :]
        v = v_ref[0, pl.ds(start, tq), :]
        s1 = _dot_nt(q1, k)
        s2 = _dot_nt(q2, k)
        if masked:
            rc = lax.broadcasted_iota(jnp.int32, (tq, tq), 0) // CHUNK
            cc = lax.broadcasted_iota(jnp.int32, (tq, tq), 1) // CHUNK
            ok = cc <= rc
            s1 = jnp.where(ok, s1, -1e30)
            s2 = jnp.where(ok, s2, -1e30)
        m1, l1, a1 = update(s1, v, m1, l1, a1)
        m2, l2, a2 = update(s2, v, m2, l2, a2)
        return m1, l1, a1, m2, l2, a2

    neg = jnp.full((tq, 1), -1e30, F32)
    zl = jnp.zeros((tq, 1), F32)
    za = jnp.zeros((tq, A_VDIM), F32)
    carry = (neg, zl, za, neg, zl, za)
    carry = lax.fori_loop(0, qi, lambda j, c: step(j, c, False), carry)
    m1, l1, a1, m2, l2, a2 = step(qi, carry, True)
    o = a1 / l1 - lam * (a2 / l2)
    o_ref[0] = _subln(o, subln_ref)


def _attn_prompt(q, k, v, lam_params, subln):
    b, s, _ = q.shape
    tq = _pick(s, 512)
    assert tq % CHUNK == 0
    kern = functools.partial(_attn_prompt_kernel, tq=tq)
    return pl.pallas_call(
        kern,
        out_shape=jax.ShapeDtypeStruct((b, s, A_V), BF16),
        grid=(b, A_HEADS, s // tq),
        in_specs=[
            pl.BlockSpec((4, A_DIM), lambda bi, h, qi: (0, 0)),
            pl.BlockSpec((1, A_VDIM), lambda bi, h, qi: (0, 0)),
            pl.BlockSpec((1, tq, A_VDIM), lambda bi, h, qi: (bi, qi, h)),
            pl.BlockSpec((1, s, A_VDIM), lambda bi, h, qi: (bi, 0, h)),
            pl.BlockSpec((1, s, A_VDIM), lambda bi, h, qi: (bi, 0, h)),
        ],
        out_specs=pl.BlockSpec((1, tq, A_VDIM), lambda bi, h, qi: (bi, qi, h)),
        compiler_params=_params(("parallel", "parallel", "arbitrary"), 48),
        name="diff_attn_prompt",
    )(lam_params, subln, q, k, v)


def _attn_sample_kernel(lam_ref, subln_ref, q_ref, kn_ref, vn_ref, ck_ref, cv_ref, o_ref):
    lam = _lam_value(lam_ref)
    for h in range(A_HEADS):
        sl = slice(h * A_VDIM, (h + 1) * A_VDIM)
        q1, q2 = _split_maps(q_ref[0, :, sl])
        kp = ck_ref[0, :, sl].astype(BF16)
        vp = cv_ref[0, :, sl].astype(BF16)
        kn = kn_ref[0, :, sl]
        vn = vn_ref[0, :, sl]

        def attend(qm):
            sp = _dot_nt(qm, kp)
            sn = _dot_nt(qm, kn)
            m = jnp.maximum(jnp.max(sp, axis=-1, keepdims=True), jnp.max(sn, axis=-1, keepdims=True))
            pp = jnp.exp(sp - m)
            pn = jnp.exp(sn - m)
            l = jnp.sum(pp, axis=-1, keepdims=True) + jnp.sum(pn, axis=-1, keepdims=True)
            return (_dot(pp.astype(BF16), vp) + _dot(pn.astype(BF16), vn)) / l

        o = attend(q1) - lam * attend(q2)
        o_ref[0, :, sl] = _subln(o, subln_ref)


def _attn_sample(q, kn, vn, cache_k, cache_v, lam_params, subln):
    b, l, _ = q.shape
    past = cache_k.shape[1]
    new = lambda bi: (bi, 0, 0)
    return pl.pallas_call(
        _attn_sample_kernel,
        out_shape=jax.ShapeDtypeStruct((b, l, A_V), BF16),
        grid=(b,),
        in_specs=[
            pl.BlockSpec((4, A_DIM), lambda bi: (0, 0)),
            pl.BlockSpec((1, A_VDIM), lambda bi: (0, 0)),
            pl.BlockSpec((1, l, A_QK), new),
            pl.BlockSpec((1, l, A_QK), new),
            pl.BlockSpec((1, l, A_V), new),
            pl.BlockSpec((1, past, A_QK), new),
            pl.BlockSpec((1, past, A_V), new),
        ],
        out_specs=pl.BlockSpec((1, l, A_V), new),
        compiler_params=_params(("parallel",), 48),
        name="diff_attn_sample",
    )(lam_params, subln, q, kn, vn, cache_k, cache_v)


def _retention_kernel(*refs, blk, has_init):
    if has_init:
        q_ref, k_ref, v_ref, gn_ref, s0_ref, o_ref, st_ref = refs
    else:
        q_ref, k_ref, v_ref, gn_ref, o_ref, st_ref = refs
    h = pl.program_id(1)
    step = pl.program_id(2)

    @pl.when(step == 0)
    def _():
        if has_init:
            st_ref[0, 0] = s0_ref[0, 0]
        else:
            st_ref[0, 0] = jnp.zeros((R_KDIM, R_VDIM), F32)

    hv = (jnp.zeros((1, 1), jnp.int32) + h).astype(F32)
    log_gamma = jnp.log(1.0 - jnp.exp(-(5.0 + hv) * math.log(2.0)))
    q = q_ref[0]
    k = k_ref[0]
    v = v_ref[0]
    ri = lax.broadcasted_iota(jnp.int32, (blk, blk), 0)
    ci = lax.broadcasted_iota(jnp.int32, (blk, blk), 1)
    decay = jnp.exp(log_gamma * jnp.abs(ri - ci).astype(F32))
    if blk > CHUNK:
        decay = jnp.where((ci // CHUNK) <= (ri // CHUNK), decay, 0.0)
    s = _dot_nt(q, k) * decay
    intra = _dot(s.astype(BF16), v)
    pos = lax.broadcasted_iota(jnp.int32, (blk, 1), 0).astype(F32)
    state = st_ref[0, 0]
    cross = _dot(q, state.astype(BF16)) * jnp.exp(log_gamma * (pos + 1.0))
    o = intra + cross
    r = lax.rsqrt(jnp.mean(o * o, axis=-1, keepdims=True) + EPS)
    o_ref[0] = ((o * r) * gn_ref[pl.ds(h, 1), :]).astype(BF16)
    k_decayed = (k.astype(F32) * jnp.exp(log_gamma * (blk - 1.0 - pos))).astype(BF16)
    st_ref[0, 0] = jnp.exp(log_gamma * float(blk)) * state + _dot_tn(k_decayed, v)


def _retention(q, k, v, gn, state0, blk):
    b, s, _ = q.shape
    has_init = state0 is not None
    kern = functools.partial(_retention_kernel, blk=blk, has_init=has_init)
    tok = lambda bi, h, j: (bi, j, h)
    st = lambda bi, h, j: (bi, h, 0, 0)
    in_specs = [
        pl.BlockSpec((1, blk, R_KDIM), tok),
        pl.BlockSpec((1, blk, R_KDIM), tok),
        pl.BlockSpec((1, blk, R_VDIM), tok),
        pl.BlockSpec((R_HEADS, R_VDIM), lambda bi, h, j: (0, 0)),
    ]
    args = [q, k, v, gn]
    if has_init:
        in_specs.append(pl.BlockSpec((1, 1, R_KDIM, R_VDIM), st))
        args.append(state0)
    return pl.pallas_call(
        kern,
        out_shape=(
            jax.ShapeDtypeStruct((b, s, R_V), BF16),
            jax.ShapeDtypeStruct((b, R_HEADS, R_KDIM, R_VDIM), F32),
        ),
        grid=(b, R_HEADS, s // blk),
        in_specs=in_specs,
        out_specs=(pl.BlockSpec((1, blk, R_VDIM), tok), pl.BlockSpec((1, 1, R_KDIM, R_VDIM), st)),
        compiler_params=_params(("parallel", "parallel", "arbitrary"), 32),
        name="retention_init" if has_init else "retention",
    )(*args)


def _swish_gate_kernel(h_ref, w_ref, orn_ref, o_ref):
    g = _dot(h_ref[...], w_ref[...])
    o_ref[...] = ((g * _sigmoid(g)) * orn_ref[...].astype(F32)).astype(BF16)


def _swish_gate(h, w, orn):
    t = h.shape[0]
    tm = _pick(t, 512)
    row = lambda i: (i, 0)
    return pl.pallas_call(
        _swish_gate_kernel,
        out_shape=jax.ShapeDtypeStruct((t, R_V), BF16),
        grid=(t // tm,),
        in_specs=[pl.BlockSpec((tm, D_MODEL), row), _resident(w.shape), pl.BlockSpec((tm, R_V), row)],
        out_specs=pl.BlockSpec((tm, R_V), row),
        compiler_params=_params(("parallel",), 40),
        name="swish_gate",
    )(h, w, orn)


def _merge_kernel(h_ref, oa_ref, orp_ref, wga_ref, wgr_ref, wa_ref, wr_ref, o_ref):
    h = h_ref[...]
    ga = _sigmoid(_dot(h, wga_ref[...]))
    gr = _sigmoid(_dot(h, wgr_ref[...]))
    o_ref[...] = (ga * _dot(oa_ref[...], wa_ref[...]) + gr * _dot(orp_ref[...], wr_ref[...])).astype(BF16)


def _merge(h, oa, orp, wga, wgr, wa, wr):
    t = h.shape[0]
    tm = _pick(t, 512)
    tn = 512
    row = lambda j, i: (i, 0)
    col = lambda j, i: (0, j)
    return pl.pallas_call(
        _merge_kernel,
        out_shape=jax.ShapeDtypeStruct((t, D_MODEL), BF16),
        grid=(D_MODEL // tn, t // tm),
        in_specs=[
            pl.BlockSpec((tm, D_MODEL), row),
            pl.BlockSpec((tm, A_V), row),
            pl.BlockSpec((tm, R_V), row),
            pl.BlockSpec((D_MODEL, tn), col),
            pl.BlockSpec((D_MODEL, tn), col),
            pl.BlockSpec((A_V, tn), col),
            pl.BlockSpec((R_V, tn), col),
        ],
        out_specs=pl.BlockSpec((tm, tn), lambda j, i: (i, j)),
        compiler_params=_params(("parallel", "parallel"), 40),
        name="merge_branches",
    )(h, oa, orp, wga, wgr, wa, wr)


def _out_proj_kernel(x_ref, m_ref, wo_ref, g_ref, wq_ref, x1_ref, h2_ref, qp_ref):
    x1 = x_ref[...] + _dot(m_ref[...], wo_ref[...])
    x1_ref[...] = x1
    r = lax.rsqrt(jnp.mean(x1 * x1, axis=-1, keepdims=True) + EPS)
    h2 = ((x1 * r) * g_ref[...]).astype(BF16)
    h2_ref[...] = h2
    qp_ref[...] = _dot(h2, wq_ref[...]).astype(BF16)


def _out_proj(x, m, wo, g, wq):
    t = x.shape[0]
    tm = _pick(t, 256)
    row = lambda i: (i, 0)
    return pl.pallas_call(
        _out_proj_kernel,
        out_shape=(
            jax.ShapeDtypeStruct((t, D_MODEL), F32),
            jax.ShapeDtypeStruct((t, D_MODEL), BF16),
            jax.ShapeDtypeStruct((t, D_MODEL), BF16),
        ),
        grid=(t // tm,),
        in_specs=[
            pl.BlockSpec((tm, D_MODEL), row),
            pl.BlockSpec((tm, D_MODEL), row),
            _resident(wo.shape),
            pl.BlockSpec((1, D_MODEL), lambda i: (0, 0)),
            _resident(wq.shape),
        ],
        out_specs=tuple(pl.BlockSpec((tm, D_MODEL), row) for _ in range(3)),
        compiler_params=_params(("parallel",), 48),
        name="out_proj",
    )(x, m, wo, g.reshape(1, D_MODEL), wq)


def _top_rows(x, n):
    rows = x.shape[0]
    ridx = lax.broadcasted_iota(jnp.int32, x.shape, 0)
    out = []
    for _ in range(n):
        m = jnp.max(x, axis=0, keepdims=True)
        out.append(m)
        first = jnp.min(jnp.where(x == m, ridx, rows), axis=0, keepdims=True)
        x = jnp.where(ridx == first, -jnp.inf, x)
    return out


_CAND_ROWS = PEER_TOPK + 7 * 8 + 8


def _peer_route_kernel(qp_ref, keys_ref, s1_ref, s2_ref, thr_ref, off_ref, top_scr, cand_scr):
    k = PEER_TOPK
    for h in range(PEER_HEADS):
        for c in range(2):
            hc = 2 * h + c
            st = _dot_nt(keys_ref[hc], qp_ref[:, hc * PEER_HALF:(hc + 1) * PEER_HALF])
            (s2_ref if c else s1_ref)[h] = st
            for r, m in enumerate(_top_rows(st, k)):
                top_scr[c, r:r + 1, :] = m
        v1 = top_scr[0]
        v2 = top_scr[1]
        cand_scr[0:k, :] = v1[0:1] + v2
        for a in range(1, 8):
            cand_scr[k + 8 * (a - 1):k + 8 * a, :] = v1[a:a + 1] + v2[0:8]
        cand_scr[k + 56:k + 64, :] = v1[8:16] + v2[0:1]
        best = _top_rows(cand_scr[...], k)
        z = jnp.zeros_like(best[0])
        for m in best:
            z = z + jnp.exp(m - best[0])
        thr_ref[h:h + 1, :] = best[k - 1]
        off_ref[h:h + 1, :] = best[0] + jnp.log(z)


def _peer_route(qp, keys):
    t = qp.shape[0]
    tm = _pick(t, 512)
    kern = _peer_route_kernel
    return pl.pallas_call(
        kern,
        out_shape=(
            jax.ShapeDtypeStruct((PEER_HEADS, N_KEYS, t), F32),
            jax.ShapeDtypeStruct((PEER_HEADS, N_KEYS, t), F32),
            jax.ShapeDtypeStruct((PEER_HEADS, t), F32),
            jax.ShapeDtypeStruct((PEER_HEADS, t), F32),
        ),
        grid=(t // tm,),
        in_specs=[
            pl.BlockSpec((tm, D_MODEL), lambda i: (i, 0)),
            pl.BlockSpec(keys.shape, lambda i: (0, 0, 0)),
        ],
        out_specs=(
            pl.BlockSpec((PEER_HEADS, N_KEYS, tm), lambda i: (0, 0, i)),
            pl.BlockSpec((PEER_HEADS, N_KEYS, tm), lambda i: (0, 0, i)),
            pl.BlockSpec((PEER_HEADS, tm), lambda i: (0, i)),
            pl.BlockSpec((PEER_HEADS, tm), lambda i: (0, i)),
        ),
        scratch_shapes=[pltpu.VMEM((2, PEER_TOPK, tm), F32), pltpu.VMEM((_CAND_ROWS, tm), F32)],
        compiler_params=_params(("parallel",), 40),
        name="peer_route",
    )(qp, keys)


def _peer_expert_kernel(h2_ref, u_ref, v_ref, s1_ref, s2_ref, thr_ref, off_ref, o_ref, act_scr, *, tm, te):
    j = pl.program_id(1)

    @pl.when(j == 0)
    def _():
        o_ref[...] = jnp.zeros_like(o_ref)

    sc = _dot_nt(u_ref[...], h2_ref[...])
    for g in range(tm // LANES):
        ls = slice(g * LANES, (g + 1) * LANES)
        for a in range(te // N_KEYS):
            w = jnp.zeros((N_KEYS, LANES), F32)
            for h in range(PEER_HEADS):
                t = s1_ref[h, a:a + 1, ls] + s2_ref[h, :, ls]
                p = jnp.exp(t - off_ref[h:h + 1, ls])
                w = w + jnp.where(t >= thr_ref[h:h + 1, ls], p, 0.0)
            x = sc[a * N_KEYS:(a + 1) * N_KEYS, ls]
            gelu = 0.5 * x * (1.0 + lax.erf(x * math.sqrt(0.5)))
            act_scr[a * N_KEYS:(a + 1) * N_KEYS, ls] = (gelu * w).astype(BF16)
    o_ref[...] += _dot_tn(act_scr[...], v_ref[...])


def _peer_experts(h2, u, v, s1, s2, thr, off):
    t = h2.shape[0]
    ne = u.shape[0]
    tm = _pick(t, 512)
    te = 1024
    kern = functools.partial(_peer_expert_kernel, tm=tm, te=te)
    return pl.pallas_call(
        kern,
        out_shape=jax.ShapeDtypeStruct((t, D_MODEL), F32),
        grid=(t // tm, ne // te),
        in_specs=[
            pl.BlockSpec((tm, D_MODEL), lambda i, j: (i, 0)),
            pl.BlockSpec((te, D_MODEL), lambda i, j: (j, 0)),
            pl.BlockSpec((te, D_MODEL), lambda i, j: (j, 0)),
            pl.BlockSpec((PEER_HEADS, te // N_KEYS, tm), lambda i, j: (0, j, i)),
            pl.BlockSpec((PEER_HEADS, N_KEYS, tm), lambda i, j: (0, 0, i)),
            pl.BlockSpec((PEER_HEADS, tm), lambda i, j: (0, i)),
            pl.BlockSpec((PEER_HEADS, tm), lambda i, j: (0, i)),
        ],
        out_specs=pl.BlockSpec((tm, D_MODEL), lambda i, j: (i, 0)),
        scratch_shapes=[pltpu.VMEM((te, tm), BF16)],
        compiler_params=_params(("parallel", "arbitrary"), 48),
        name="peer_experts",
    )(h2, u, v, s1, s2, thr, off)


def _final_kernel(x_ref, p_ref, g_ref, o_ref):
    x = x_ref[...] + p_ref[...]
    r = lax.rsqrt(jnp.mean(x * x, axis=-1, keepdims=True) + EPS)
    o_ref[...] = (x * r) * g_ref[...]


def _final(x1, p, g):
    t, d = x1.shape
    tm = _pick(t, 512)
    row = lambda i: (i, 0)
    return pl.pallas_call(
        _final_kernel,
        out_shape=jax.ShapeDtypeStruct((t, d), F32),
        grid=(t // tm,),
        in_specs=[pl.BlockSpec((tm, d), row), pl.BlockSpec((tm, d), row), pl.BlockSpec((1, d), lambda i: (0, 0))],
        out_specs=pl.BlockSpec((tm, d), row),
        compiler_params=_params(("parallel",), 40),
        name="final_norm",
    )(x1, p, g.reshape(1, d))


def _stream(x, w, past):
    b, l, d = x.shape
    t = b * l
    x2 = x.reshape(t, d)
    h = _rmsnorm_bf16(x2, w["norm_mix"])
    qa, kf, vf, kb, vb = _proj_attn(h, w["w_attn"])
    if past is None:
        qr, kr, vr = _proj_ret(h, w["w_ret"], l, 0)
    else:
        qr, kr, vr = _proj_ret(h, w["w_ret"], l, past[3])
    as3 = lambda a: a.reshape(b, l, a.shape[-1])
    if past is None:
        oa = _attn_prompt(as3(qa), as3(kb), as3(vb), w["lam"], w["subln"])
        blk = _pick(l, 256)
        orn, state = _retention(as3(qr), as3(kr), as3(vr), w["ret_gn"], None, blk)
    else:
        cache_k, cache_v, state0, _ = past
        oa = _attn_sample(as3(qa), as3(kb), as3(vb), cache_k, cache_v, w["lam"], w["subln"])
        orn, state = _retention(as3(qr), as3(kr), as3(vr), w["ret_gn"], state0, l)
    orp = _swish_gate(h, w["w_gs"], orn.reshape(t, R_V))
    m = _merge(h, oa.reshape(t, A_V), orp, w["w_ga"], w["w_gr"], w["w_a"], w["w_r"])
    x1, h2, qp = _out_proj(x2, m, w["w_o"], w["norm_ffn"], w["w_q"])
    s1, s2, thr, off = _peer_route(qp, w["keys"])
    p = _peer_experts(h2, w["u"], w["v"], s1, s2, thr, off)
    y = _final(x1, p, w["norm_final"])
    k_rows = kf.reshape(1, b, l, A_HEADS, 2 * A_DIM)
    v_rows = vf.reshape(1, b, l, A_HEADS, A_VDIM)
    return y.reshape(b, l, d), k_rows, v_rows, state[None]


def kernel(x_prompt, x_sample, cache_k, cache_v, state_ret, norm_mix, w_in, lambda_q1, lambda_k1, lambda_q2, lambda_k2, attn_subln, ret_gn, w_branch_a, w_branch_r, w_out, norm_ffn, peer_wq, peer_keys, peer_u, peer_v, norm_final):
    assert w_in.shape[0] == 1, "single layer"
    wi = w_in[0]
    o_ret = 2 * A_QK + A_V
    o_gs = o_ret + 2 * R_QK + R_V
    o_ga = o_gs + R_V
    o_gr = o_ga + D_MODEL
    w = {
        "norm_mix": norm_mix[0],
        "w_attn": wi[:, :o_ret].astype(BF16),
        "w_ret": wi[:, o_ret:o_gs].astype(BF16),
        "w_gs": wi[:, o_gs:o_ga].astype(BF16),
        "w_ga": wi[:, o_ga:o_gr].astype(BF16),
        "w_gr": wi[:, o_gr:].astype(BF16),
        "lam": jnp.stack([lambda_q1[0], lambda_k1[0], lambda_q2[0], lambda_k2[0]]).astype(F32),
        "subln": attn_subln[0].reshape(1, A_VDIM),
        "ret_gn": ret_gn[0],
        "w_a": w_branch_a[0].astype(BF16),
        "w_r": w_branch_r[0].astype(BF16),
        "w_o": w_out[0].astype(BF16),
        "norm_ffn": norm_ffn[0],
        "w_q": peer_wq[0].astype(BF16),
        "keys": peer_keys[0].reshape(2 * PEER_HEADS, N_KEYS, PEER_HALF).astype(BF16),
        "u": peer_u[0].astype(BF16),
        "v": peer_v[0].astype(BF16),
        "norm_final": norm_final,
    }
    bd, past_len = cache_k.shape[1], cache_k.shape[2]
    past = (
        cache_k[0].reshape(bd, past_len, A_QK),
        cache_v[0].reshape(bd, past_len, A_V),
        state_ret[0],
        past_len,
    )
    y_p, k_p, v_p, s_p = _stream(x_prompt, w, None)
    y_s, k_s, v_s, s_s = _stream(x_sample, w, past)
    return (y_p, y_s, k_p, v_p, s_p, k_s, v_s, s_s)
```

```python
import functools
import math

import jax
import jax.numpy as jnp
from jax import lax
from jax.experimental import pallas as pl
from jax.experimental.pallas import tpu as pltpu

D_MODEL = 2048
CHUNK = 64
EPS = 1e-6
A_HEADS = 8
A_DIM = 64
A_VDIM = 2 * A_DIM
R_HEADS = 8
R_KDIM = 128
R_VDIM = 256
PEER_HEADS = 8
N_KEYS = 128
PEER_HALF = 128
PEER_TOPK = 16
A_QK = A_HEADS * 2 * A_DIM
A_V = A_HEADS * A_VDIM
R_QK = R_HEADS * R_KDIM
R_V = R_HEADS * R_VDIM
LAM_INIT = 0.8 - 0.6 * math.exp(-0.3 * 0)
ROPE_BASE = 10000.0
LOG2E = math.log2(math.e)

LANES = 128
MIB = 1024 * 1024

F32 = jnp.float32
BF16 = jnp.bfloat16


def _dot(a, b):
    return jnp.dot(a, b, preferred_element_type=F32)


def _dot_nt(a, b):
    return lax.dot_general(a, b, (((1,), (1,)), ((), ())), preferred_element_type=F32)


def _dot_tn(a, b):
    return lax.dot_general(a, b, (((0,), (0,)), ((), ())), preferred_element_type=F32)


def _sigmoid(x):
    return 1.0 / (1.0 + jnp.exp(-x))


def _params(semantics, vmem_mib):
    return pltpu.CompilerParams(dimension_semantics=semantics, vmem_limit_bytes=int(vmem_mib * MIB))


def _pick(n, pref):
    return pref if n % pref == 0 else n


def _resident(shape):
    nd = len(shape)
    return pl.BlockSpec(shape, lambda *_: (0,) * nd, pipeline_mode=pl.Buffered(1))


def _rmsnorm_kernel(x_ref, g_ref, o_ref):
    x = x_ref[...]
    r = lax.rsqrt(jnp.mean(x * x, axis=-1, keepdims=True) + EPS)
    o_ref[...] = ((x * r) * g_ref[...]).astype(o_ref.dtype)


def _rmsnorm_bf16(x, g):
    t, d = x.shape
    tm = _pick(t, 512)
    return pl.pallas_call(
        _rmsnorm_kernel,
        out_shape=jax.ShapeDtypeStruct((t, d), BF16),
        grid=(t // tm,),
        in_specs=[pl.BlockSpec((tm, d), lambda i: (i, 0)), pl.BlockSpec((1, d), lambda i: (0, 0))],
        out_specs=pl.BlockSpec((tm, d), lambda i: (i, 0)),
        compiler_params=_params(("parallel",), 32),
        name="rmsnorm_mix",
    )(x, g.reshape(1, d))


def _proj_attn_kernel(h_ref, w_ref, q_ref, kf_ref, vf_ref, kb_ref, vb_ref):
    h = h_ref[...]
    q = _dot(h, w_ref[:, 0:A_QK])
    q_ref[...] = (q * (A_DIM ** -0.5 * LOG2E)).astype(BF16)
    k = _dot(h, w_ref[:, A_QK:2 * A_QK])
    kf_ref[...] = k
    kb_ref[...] = k.astype(BF16)
    v = _dot(h, w_ref[:, 2 * A_QK:2 * A_QK + A_V])
    vf_ref[...] = v
    vb_ref[...] = v.astype(BF16)


def _proj_attn(h, w):
    t = h.shape[0]
    tm = _pick(t, 512)
    row = lambda i: (i, 0)
    n = A_QK
    return pl.pallas_call(
        _proj_attn_kernel,
        out_shape=(
            jax.ShapeDtypeStruct((t, n), BF16),
            jax.ShapeDtypeStruct((t, n), F32),
            jax.ShapeDtypeStruct((t, n), F32),
            jax.ShapeDtypeStruct((t, n), BF16),
            jax.ShapeDtypeStruct((t, n), BF16),
        ),
        grid=(t // tm,),
        in_specs=[pl.BlockSpec((tm, D_MODEL), row), _resident(w.shape)],
        out_specs=tuple(pl.BlockSpec((tm, n), row) for _ in range(5)),
        compiler_params=_params(("parallel",), 48),
        name="proj_attn",
    )(h, w)


def _proj_ret_kernel(h_ref, w_ref, q_ref, k_ref, v_ref, *, tm, pos_period, pos_offset):
    i = pl.program_id(0)
    h = h_ref[...]
    half = R_KDIM // 2
    row = lax.broadcasted_iota(jnp.int32, (tm, R_KDIM), 0) + i * tm
    pos = (lax.rem(row, pos_period) + pos_offset).astype(F32)
    lane = lax.broadcasted_iota(jnp.int32, (tm, R_KDIM), 1)
    fidx = lax.rem(lane, half).astype(F32)
    freq = jnp.exp(fidx * (-math.log(ROPE_BASE) / half))
    ang = pos * freq
    cos = jnp.cos(ang)
    sin = jnp.sin(ang)
    sin_signed = jnp.where(lane < half, -sin, sin)

    def rotate_into(y, o_ref):
        for hh in range(R_HEADS):
            sl = slice(hh * R_KDIM, (hh + 1) * R_KDIM)
            yh = y[:, sl]
            o_ref[:, sl] = (yh * cos + pltpu.roll(yh, half, 1) * sin_signed).astype(BF16)

    rotate_into(_dot(h, w_ref[:, 0:R_QK]), q_ref)
    rotate_into(_dot(h, w_ref[:, R_QK:2 * R_QK]) * (R_KDIM ** -0.5), k_ref)
    v_ref[...] = _dot(h, w_ref[:, 2 * R_QK:2 * R_QK + R_V]).astype(BF16)


def _proj_ret(h, w, pos_period, pos_offset):
    t = h.shape[0]
    tm = _pick(t, 512)
    if pos_period < tm:
        assert tm % pos_period == 0
    else:
        assert pos_period % tm == 0
    row = lambda i: (i, 0)
    kern = functools.partial(_proj_ret_kernel, tm=tm, pos_period=pos_period, pos_offset=pos_offset)
    return pl.pallas_call(
        kern,
        out_shape=(
            jax.ShapeDtypeStruct((t, R_QK), BF16),
            jax.ShapeDtypeStruct((t, R_QK), BF16),
            jax.ShapeDtypeStruct((t, R_V), BF16),
        ),
        grid=(t // tm,),
        in_specs=[pl.BlockSpec((tm, D_MODEL), row), _resident(w.shape)],
        out_specs=(pl.BlockSpec((tm, R_QK), row), pl.BlockSpec((tm, R_QK), row), pl.BlockSpec((tm, R_V), row)),
        compiler_params=_params(("parallel",), 48),
        name="proj_ret",
    )(h, w)


def _lam_value(lam_ref):
    p = lam_ref[...]
    a = jnp.sum(p[0:1] * p[1:2], axis=-1, keepdims=True)
    b = jnp.sum(p[2:3] * p[3:4], axis=-1, keepdims=True)
    return jnp.exp(a) - jnp.exp(b) + LAM_INIT


def _split_maps(q):
    lane = lax.broadcasted_iota(jnp.int32, q.shape, 1)
    zero = jnp.zeros_like(q)
    return jnp.where(lane < A_DIM, q, zero), jnp.where(lane >= A_DIM, q, zero)


def _subln(o, subln_ref):
    r = lax.rsqrt(jnp.mean(o * o, axis=-1, keepdims=True) + EPS)
    return (((o * r) * subln_ref[...]) * (1.0 - LAM_INIT)).astype(BF16)


ATTN_ROWS = 32


def _attn_prompt_kernel(lam_ref, subln_ref, q_ref, k_ref, v_ref, o_ref,
                        s_scr, p_scr, m_scr, l_scr, al_scr, acc_scr, *, tq):
    qi = pl.program_id(2)
    lam = _lam_value(lam_ref)
    q1, q2 = _split_maps(q_ref[0])
    m_scr[...] = jnp.full(m_scr.shape, -1e30, F32)
    l_scr[...] = jnp.zeros(l_scr.shape, F32)
    acc_scr[...] = jnp.zeros(acc_scr.shape, F32)
    lane = lax.broadcasted_iota(jnp.int32, (ATTN_ROWS, LANES), 1)

    def softmax_rows(mp, rc, masked):
        sidx = mp
        rows = slice(rc * ATTN_ROWS, (rc + 1) * ATTN_ROWS)
        visible = ((rc * ATTN_ROWS) // CHUNK + 1) * CHUNK if masked else tq

        def load(c):
            lo = c * LANES
            s = s_scr[sidx, rows, lo:lo + LANES]
            if lo + LANES > visible:
                s = jnp.where(lane < visible - lo, s, -1e30)
            return s

        n_live = -(-visible // LANES)
        row_max = functools.reduce(jnp.maximum, [load(c) for c in range(n_live)])
        m_old = m_scr[mp, rows, :]
        m_new = jnp.maximum(m_old, jnp.max(row_max, axis=-1, keepdims=True))
        alpha = jnp.exp2(m_old - m_new)
        row_sum = None
        for c in range(tq // LANES):
            lo = c * LANES
            if c >= n_live:
                p_scr[mp, rows, lo:lo + LANES] = jnp.zeros((ATTN_ROWS, LANES), BF16)
                continue
            p = jnp.exp2(load(c) - m_new)
            row_sum = p if row_sum is None else row_sum + p
            p_scr[mp, rows, lo:lo + LANES] = p.astype(BF16)
        l_scr[mp, rows, :] = alpha * l_scr[mp, rows, :] + jnp.sum(row_sum, axis=-1, keepdims=True)
        m_scr[mp, rows, :] = m_new
        al_scr[mp, rows, :] = alpha

    def step(j, masked):
        start = pl.multiple_of(j * tq, tq)
        k = k_ref[0, pl.ds(start, tq), :]
        v = v_ref[0, pl.ds(start, tq), :]
        s_scr[0] = _dot_nt(q1, k)
        s_scr[1] = _dot_nt(q2, k)
        for mp in range(2):
            for rc in range(tq // ATTN_ROWS):
                softmax_rows(mp, rc, masked)
            acc_scr[mp] = al_scr[mp] * acc_scr[mp] + _dot(p_scr[mp], v)

    def full_step(j, carry):
        step(j, False)
        return carry

    lax.fori_loop(0, qi, full_step, 0)
    step(qi, True)
    o = acc_scr[0] / l_scr[0] - lam * (acc_scr[1] / l_scr[1])
    o_ref[0] = _subln(o, subln_ref)


def _attn_prompt(q, k, v, lam_params, subln):
    b, s, _ = q.shape
    tq = _pick(s, 512)
    assert tq % CHUNK == 0 and tq % LANES == 0 and CHUNK % ATTN_ROWS == 0
    kern = functools.partial(_attn_prompt_kernel, tq=tq)
    return pl.pallas_call(
        kern,
        out_shape=jax.ShapeDtypeStruct((b, s, A_V), BF16),
        grid=(b, A_HEADS, s // tq),
        in_specs=[
            pl.BlockSpec((4, A_DIM), lambda bi, h, qi: (0, 0)),
            pl.BlockSpec((1, A_VDIM), lambda bi, h, qi: (0, 0)),
            pl.BlockSpec((1, tq, A_VDIM), lambda bi, h, qi: (bi, qi, h)),
            pl.BlockSpec((1, s, A_VDIM), lambda bi, h, qi: (bi, 0, h)),
            pl.BlockSpec((1, s, A_VDIM), lambda bi, h, qi: (bi, 0, h)),
        ],
        out_specs=pl.BlockSpec((1, tq, A_VDIM), lambda bi, h, qi: (bi, qi, h)),
        scratch_shapes=[
            pltpu.VMEM((2, tq, tq), F32),
            pltpu.VMEM((2, tq, tq), BF16),
            pltpu.VMEM((2, tq, LANES), F32),
            pltpu.VMEM((2, tq, LANES), F32),
            pltpu.VMEM((2, tq, LANES), F32),
            pltpu.VMEM((2, tq, A_VDIM), F32),
        ],
        compiler_params=_params(("parallel", "parallel", "arbitrary"), 48),
        name="diff_attn_prompt",
    )(lam_params, subln, q, k, v)


def _attn_sample_kernel(lam_ref, subln_ref, q_ref, kn_ref, vn_ref, ck_ref, cv_ref, o_ref):
    lam = _lam_value(lam_ref)
    for h in range(A_HEADS):
        sl = slice(h * A_VDIM, (h + 1) * A_VDIM)
        q1, q2 = _split_maps(q_ref[0, :, sl])
        kp = ck_ref[0, :, sl].astype(BF16)
        vp = cv_ref[0, :, sl].astype(BF16)
        kn = kn_ref[0, :, sl]
        vn = vn_ref[0, :, sl]

        def attend(qm):
            sp = _dot_nt(qm, kp)
            sn = _dot_nt(qm, kn)
            m = jnp.maximum(jnp.max(sp, axis=-1, keepdims=True), jnp.max(sn, axis=-1, keepdims=True))
            pp = jnp.exp2(sp - m)
            pn = jnp.exp2(sn - m)
            l = jnp.sum(pp, axis=-1, keepdims=True) + jnp.sum(pn, axis=-1, keepdims=True)
            return (_dot(pp.astype(BF16), vp) + _dot(pn.astype(BF16), vn)) / l

        o = attend(q1) - lam * attend(q2)
        o_ref[0, :, sl] = _subln(o, subln_ref)


def _attn_sample(q, kn, vn, cache_k, cache_v, lam_params, subln):
    b, l, _ = q.shape
    past = cache_k.shape[1]
    new = lambda bi: (bi, 0, 0)
    return pl.pallas_call(
        _attn_sample_kernel,
        out_shape=jax.ShapeDtypeStruct((b, l, A_V), BF16),
        grid=(b,),
        in_specs=[
            pl.BlockSpec((4, A_DIM), lambda bi: (0, 0)),
            pl.BlockSpec((1, A_VDIM), lambda bi: (0, 0)),
            pl.BlockSpec((1, l, A_QK), new),
            pl.BlockSpec((1, l, A_QK), new),
            pl.BlockSpec((1, l, A_V), new),
            pl.BlockSpec((1, past, A_QK), new),
            pl.BlockSpec((1, past, A_V), new),
        ],
        out_specs=pl.BlockSpec((1, l, A_V), new),
        compiler_params=_params(("parallel",), 48),
        name="diff_attn_sample",
    )(lam_params, subln, q, kn, vn, cache_k, cache_v)


def _retention_kernel(*refs, blk, has_init):
    if has_init:
        q_ref, k_ref, v_ref, gn_ref, s0_ref, o_ref, st_ref = refs
    else:
        q_ref, k_ref, v_ref, gn_ref, o_ref, st_ref = refs
    h = pl.program_id(1)
    step = pl.program_id(2)

    @pl.when(step == 0)
    def _():
        if has_init:
            st_ref[0, 0] = s0_ref[0, 0]
        else:
            st_ref[0, 0] = jnp.zeros((R_KDIM, R_VDIM), F32)

    hv = (jnp.zeros((1, 1), jnp.int32) + h).astype(F32)
    log_gamma = jnp.log(1.0 - jnp.exp(-(5.0 + hv) * math.log(2.0)))
    q = q_ref[0]
    k = k_ref[0]
    v = v_ref[0]
    ri = lax.broadcasted_iota(jnp.int32, (blk, blk), 0)
    ci = lax.broadcasted_iota(jnp.int32, (blk, blk), 1)
    decay = jnp.exp(log_gamma * jnp.abs(ri - ci).astype(F32))
    if blk > CHUNK:
        decay = jnp.where((ci // CHUNK) <= (ri // CHUNK), decay, 0.0)
    s = _dot_nt(q, k) * decay
    intra = _dot(s.astype(BF16), v)
    pos = lax.broadcasted_iota(jnp.int32, (blk, 1), 0).astype(F32)
    state = st_ref[0, 0]
    cross = _dot(q, state.astype(BF16)) * jnp.exp(log_gamma * (pos + 1.0))
    o = intra + cross
    r = lax.rsqrt(jnp.mean(o * o, axis=-1, keepdims=True) + EPS)
    o_ref[0] = ((o * r) * gn_ref[pl.ds(h, 1), :]).astype(BF16)
    k_decayed = (k.astype(F32) * jnp.exp(log_gamma * (blk - 1.0 - pos))).astype(BF16)
    st_ref[0, 0] = jnp.exp(log_gamma * float(blk)) * state + _dot_tn(k_decayed, v)


def _retention(q, k, v, gn, state0, blk):
    b, s, _ = q.shape
    has_init = state0 is not None
    kern = functools.partial(_retention_kernel, blk=blk, has_init=has_init)
    tok = lambda bi, h, j: (bi, j, h)
    st = lambda bi, h, j: (bi, h, 0, 0)
    in_specs = [
        pl.BlockSpec((1, blk, R_KDIM), tok),
        pl.BlockSpec((1, blk, R_KDIM), tok),
        pl.BlockSpec((1, blk, R_VDIM), tok),
        pl.BlockSpec((R_HEADS, R_VDIM), lambda bi, h, j: (0, 0)),
    ]
    args = [q, k, v, gn]
    if has_init:
        in_specs.append(pl.BlockSpec((1, 1, R_KDIM, R_VDIM), st))
        args.append(state0)
    return pl.pallas_call(
        kern,
        out_shape=(
            jax.ShapeDtypeStruct((b, s, R_V), BF16),
            jax.ShapeDtypeStruct((b, R_HEADS, R_KDIM, R_VDIM), F32),
        ),
        grid=(b, R_HEADS, s // blk),
        in_specs=in_specs,
        out_specs=(pl.BlockSpec((1, blk, R_VDIM), tok), pl.BlockSpec((1, 1, R_KDIM, R_VDIM), st)),
        compiler_params=_params(("parallel", "parallel", "arbitrary"), 32),
        name="retention_init" if has_init else "retention",
    )(*args)


def _swish_gate_kernel(h_ref, w_ref, orn_ref, o_ref):
    g = _dot(h_ref[...], w_ref[...])
    o_ref[...] = ((g * _sigmoid(g)) * orn_ref[...].astype(F32)).astype(BF16)


def _swish_gate(h, w, orn):
    t = h.shape[0]
    tm = _pick(t, 512)
    row = lambda i: (i, 0)
    return pl.pallas_call(
        _swish_gate_kernel,
        out_shape=jax.ShapeDtypeStruct((t, R_V), BF16),
        grid=(t // tm,),
        in_specs=[pl.BlockSpec((tm, D_MODEL), row), _resident(w.shape), pl.BlockSpec((tm, R_V), row)],
        out_specs=pl.BlockSpec((tm, R_V), row),
        compiler_params=_params(("parallel",), 40),
        name="swish_gate",
    )(h, w, orn)


def _merge_kernel(h_ref, oa_ref, orp_ref, wga_ref, wgr_ref, wa_ref, wr_ref, o_ref):
    h = h_ref[...]
    ga = _sigmoid(_dot(h, wga_ref[...]))
    gr = _sigmoid(_dot(h, wgr_ref[...]))
    o_ref[...] = (ga * _dot(oa_ref[...], wa_ref[...]) + gr * _dot(orp_ref[...], wr_ref[...])).astype(BF16)


def _merge(h, oa, orp, wga, wgr, wa, wr):
    t = h.shape[0]
    tm = _pick(t, 512)
    tn = 512
    row = lambda j, i: (i, 0)
    col = lambda j, i: (0, j)
    return pl.pallas_call(
        _merge_kernel,
        out_shape=jax.ShapeDtypeStruct((t, D_MODEL), BF16),
        grid=(D_MODEL // tn, t // tm),
        in_specs=[
            pl.BlockSpec((tm, D_MODEL), row),
            pl.BlockSpec((tm, A_V), row),
            pl.BlockSpec((tm, R_V), row),
            pl.BlockSpec((D_MODEL, tn), col),
            pl.BlockSpec((D_MODEL, tn), col),
            pl.BlockSpec((A_V, tn), col),
            pl.BlockSpec((R_V, tn), col),
        ],
        out_specs=pl.BlockSpec((tm, tn), lambda j, i: (i, j)),
        compiler_params=_params(("parallel", "parallel"), 40),
        name="merge_branches",
    )(h, oa, orp, wga, wgr, wa, wr)


def _out_proj_kernel(x_ref, m_ref, wo_ref, g_ref, wq_ref, x1_ref, h2_ref, qp_ref):
    x1 = x_ref[...] + _dot(m_ref[...], wo_ref[...])
    x1_ref[...] = x1
    r = lax.rsqrt(jnp.mean(x1 * x1, axis=-1, keepdims=True) + EPS)
    h2 = ((x1 * r) * g_ref[...]).astype(BF16)
    h2_ref[...] = h2
    qp_ref[...] = _dot(h2, wq_ref[...]).astype(BF16)


def _out_proj(x, m, wo, g, wq):
    t = x.shape[0]
    tm = _pick(t, 256)
    row = lambda i: (i, 0)
    return pl.pallas_call(
        _out_proj_kernel,
        out_shape=(
            jax.ShapeDtypeStruct((t, D_MODEL), F32),
            jax.ShapeDtypeStruct((t, D_MODEL), BF16),
            jax.ShapeDtypeStruct((t, D_MODEL), BF16),
        ),
        grid=(t // tm,),
        in_specs=[
            pl.BlockSpec((tm, D_MODEL), row),
            pl.BlockSpec((tm, D_MODEL), row),
            _resident(wo.shape),
            pl.BlockSpec((1, D_MODEL), lambda i: (0, 0)),
            _resident(wq.shape),
        ],
        out_specs=tuple(pl.BlockSpec((tm, D_MODEL), row) for _ in range(3)),
        compiler_params=_params(("parallel",), 48),
        name="out_proj",
    )(x, m, wo, g.reshape(1, D_MODEL), wq)


def _top_rows(x, n):
    rows = x.shape[0]
    ridx = lax.broadcasted_iota(jnp.int32, x.shape, 0)
    out = []
    for _ in range(n):
        m = jnp.max(x, axis=0, keepdims=True)
        out.append(m)
        first = jnp.min(jnp.where(x == m, ridx, rows), axis=0, keepdims=True)
        x = jnp.where(ridx == first, -jnp.inf, x)
    return out


_TOP_N = PEER_TOPK + 1
_TOP_ROWS = 24
_CAND_ROWS = _TOP_ROWS + 7 * 8 + (_TOP_ROWS - 8)


def _peer_route_kernel(qp_ref, keys_ref, s1_ref, s2_ref, thr_ref, top_scr, cand_scr):
    k = PEER_TOPK
    top_scr[:, _TOP_N:, :] = jnp.full((2, _TOP_ROWS - _TOP_N, top_scr.shape[2]), -jnp.inf, F32)
    for h in range(PEER_HEADS):
        for c in range(2):
            hc = 2 * h + c
            st = _dot_nt(keys_ref[hc], qp_ref[:, hc * PEER_HALF:(hc + 1) * PEER_HALF])
            (s2_ref if c else s1_ref)[h] = st
            for r, m in enumerate(_top_rows(st, _TOP_N)):
                top_scr[c, r:r + 1, :] = m
        v1 = top_scr[0]
        v2 = top_scr[1]
        cand_scr[0:_TOP_ROWS, :] = v1[0:1] + v2
        for a in range(1, 8):
            cand_scr[_TOP_ROWS + 8 * (a - 1):_TOP_ROWS + 8 * a, :] = v1[a:a + 1] + v2[0:8]
        cand_scr[_TOP_ROWS + 56:, :] = v1[8:_TOP_ROWS] + v2[0:1]
        best = _top_rows(cand_scr[...], _TOP_N)
        z = jnp.zeros_like(best[0])
        for m in best[:k]:
            z = z + jnp.exp(m - best[0])
        off = best[0] + jnp.log(z)
        thr_ref[h:h + 1, :] = (0.5 * (best[k - 1] + best[k]) - off) * LOG2E - 1.0
        s1_ref[h] = (s1_ref[h] - off) * LOG2E - 1.0
        s2_ref[h] = s2_ref[h] * LOG2E


def _peer_route(qp, keys):
    t = qp.shape[0]
    tm = _pick(t, 512)
    return pl.pallas_call(
        _peer_route_kernel,
        out_shape=(
            jax.ShapeDtypeStruct((PEER_HEADS, N_KEYS, t), F32),
            jax.ShapeDtypeStruct((PEER_HEADS, N_KEYS, t), F32),
            jax.ShapeDtypeStruct((PEER_HEADS, t), F32),
        ),
        grid=(t // tm,),
        in_specs=[
            pl.BlockSpec((tm, D_MODEL), lambda i: (i, 0)),
            pl.BlockSpec(keys.shape, lambda i: (0, 0, 0)),
        ],
        out_specs=(
            pl.BlockSpec((PEER_HEADS, N_KEYS, tm), lambda i: (0, 0, i)),
            pl.BlockSpec((PEER_HEADS, N_KEYS, tm), lambda i: (0, 0, i)),
            pl.BlockSpec((PEER_HEADS, tm), lambda i: (0, i)),
        ),
        scratch_shapes=[pltpu.VMEM((2, _TOP_ROWS, tm), F32), pltpu.VMEM((_CAND_ROWS, tm), F32)],
        compiler_params=_params(("parallel",), 40),
        name="peer_route",
    )(qp, keys)


GATE_ROWS = 16


def _ordered_zero(x):
    bits = lax.bitcast_convert_type(x, jnp.int32)
    return lax.shift_right_logical(lax.shift_right_logical(bits, 16), 16).astype(F32)


def _peer_expert_kernel(h2_ref, u_ref, v_ref, s1_first_ref, s1_next_ref, s2_ref, thr_ref, o_ref,
                        w_scr, act_scr, *, tm, te, n_tiles):
    j = pl.program_id(1)
    n_key1 = te // N_KEYS

    def build_gates(s1_ref, slot, key1_rows):
        link = jnp.zeros((8, LANES), F32)
        for a in key1_rows:
            for g in range(tm // LANES):
                ls = slice(g * LANES, (g + 1) * LANES)
                for r in range(N_KEYS // GATE_ROWS):
                    k2 = slice(r * GATE_ROWS, (r + 1) * GATE_ROWS)
                    w = jnp.concatenate([link] * (GATE_ROWS // 8), axis=0)
                    for h in range(PEER_HEADS):
                        t = s1_ref[h, a:a + 1, ls] + s2_ref[h, k2, ls]
                        w = w + jnp.where(t >= thr_ref[h:h + 1, ls], jnp.exp2(t), 0.0)
                    w_scr[slot, a * N_KEYS + r * GATE_ROWS:a * N_KEYS + (r + 1) * GATE_ROWS, ls] = w
                    link = _ordered_zero(w[0:8])
        return link

    def run_experts(slot, tie):
        sc = _dot_nt(u_ref[...], h2_ref[...])
        for a in range(n_key1):
            for g in range(tm // LANES):
                ls = slice(g * LANES, (g + 1) * LANES)
                rows = slice(a * N_KEYS, (a + 1) * N_KEYS)
                x = sc[rows, ls]
                act = (x * (1.0 + lax.erf(x * math.sqrt(0.5)))) * w_scr[slot, rows, ls]
                if tie is not None and a == n_key1 - 1 and g == tm // LANES - 1:
                    act = jnp.concatenate([act[0:8] + tie, act[8:]], axis=0)
                act_scr[rows, ls] = act.astype(BF16)
        o_ref[...] += _dot_tn(act_scr[...], v_ref[...])

    @pl.when(j == 0)
    def _():
        o_ref[...] = jnp.zeros_like(o_ref)
        build_gates(s1_first_ref, 0, range(n_key1))

    for parity in range(2):
        @pl.when((lax.rem(j, 2) == parity) & (j < n_tiles - 1))
        def _():
            tie = build_gates(s1_next_ref, 1 - parity, range(0, n_key1 // 2))
            run_experts(parity, tie)
            build_gates(s1_next_ref, 1 - parity, range(n_key1 // 2, n_key1))

        if (n_tiles - 1) % 2 == parity:
            @pl.when(j == n_tiles - 1)
            def _():
                run_experts(parity, None)


def _peer_experts(h2, u, v, s1, s2, thr):
    t = h2.shape[0]
    ne = u.shape[0]
    tm = _pick(t, 512)
    te = 1024
    n_tiles = ne // te
    n_key1 = te // N_KEYS
    assert n_tiles >= 2 and n_key1 % 8 == 0
    kern = functools.partial(_peer_expert_kernel, tm=tm, te=te, n_tiles=n_tiles)
    return pl.pallas_call(
        kern,
        out_shape=jax.ShapeDtypeStruct((t, D_MODEL), F32),
        grid=(t // tm, n_tiles),
        in_specs=[
            pl.BlockSpec((tm, D_MODEL), lambda i, j: (i, 0)),
            pl.BlockSpec((te, D_MODEL), lambda i, j: (j, 0)),
            pl.BlockSpec((te, D_MODEL), lambda i, j: (j, 0)),
            pl.BlockSpec((PEER_HEADS, n_key1, tm), lambda i, j: (0, 0, i)),
            pl.BlockSpec((PEER_HEADS, n_key1, tm), lambda i, j: (0, jnp.minimum(j + 1, n_tiles - 1), i)),
            pl.BlockSpec((PEER_HEADS, N_KEYS, tm), lambda i, j: (0, 0, i)),
            pl.BlockSpec((PEER_HEADS, tm), lambda i, j: (0, i)),
        ],
        out_specs=pl.BlockSpec((tm, D_MODEL), lambda i, j: (i, 0)),
        scratch_shapes=[pltpu.VMEM((2, te, tm), F32), pltpu.VMEM((te, tm), BF16)],
        compiler_params=_params(("parallel", "arbitrary"), 56),
        name="peer_experts",
    )(h2, u, v, s1, s1, s2, thr)


def _final_kernel(x_ref, p_ref, g_ref, o_ref):
    x = x_ref[...] + p_ref[...]
    r = lax.rsqrt(jnp.mean(x * x, axis=-1, keepdims=True) + EPS)
    o_ref[...] = (x * r) * g_ref[...]


def _final(x1, p, g):
    t, d = x1.shape
    tm = _pick(t, 512)
    row = lambda i: (i, 0)
    return pl.pallas_call(
        _final_kernel,
        out_shape=jax.ShapeDtypeStruct((t, d), F32),
        grid=(t // tm,),
        in_specs=[pl.BlockSpec((tm, d), row), pl.BlockSpec((tm, d), row), pl.BlockSpec((1, d), lambda i: (0, 0))],
        out_specs=pl.BlockSpec((tm, d), row),
        compiler_params=_params(("parallel",), 40),
        name="final_norm",
    )(x1, p, g.reshape(1, d))


def _stream(x, w, past):
    b, l, d = x.shape
    t = b * l
    x2 = x.reshape(t, d)
    h = _rmsnorm_bf16(x2, w["norm_mix"])
    qa, kf, vf, kb, vb = _proj_attn(h, w["w_attn"])
    if past is None:
        qr, kr, vr = _proj_ret(h, w["w_ret"], l, 0)
    else:
        qr, kr, vr = _proj_ret(h, w["w_ret"], l, past[3])
    as3 = lambda a: a.reshape(b, l, a.shape[-1])
    if past is None:
        oa = _attn_prompt(as3(qa), as3(kb), as3(vb), w["lam"], w["subln"])
        blk = _pick(l, 512)
        orn, state = _retention(as3(qr), as3(kr), as3(vr), w["ret_gn"], None, blk)
    else:
        cache_k, cache_v, state0, _ = past
        oa = _attn_sample(as3(qa), as3(kb), as3(vb), cache_k, cache_v, w["lam"], w["subln"])
        orn, state = _retention(as3(qr), as3(kr), as3(vr), w["ret_gn"], state0, l)
    orp = _swish_gate(h, w["w_gs"], orn.reshape(t, R_V))
    m = _merge(h, oa.reshape(t, A_V), orp, w["w_ga"], w["w_gr"], w["w_a"], w["w_r"])
    x1, h2, qp = _out_proj(x2, m, w["w_o"], w["norm_ffn"], w["w_q"])
    s1, s2, thr = _peer_route(qp, w["keys"])
    p = _peer_experts(h2, w["u"], w["v"], s1, s2, thr)
    y = _final(x1, p, w["norm_final"])
    k_rows = kf.reshape(1, b, l, A_HEADS, 2 * A_DIM)
    v_rows = vf.reshape(1, b, l, A_HEADS, A_VDIM)
    return y.reshape(b, l, d), k_rows, v_rows, state[None]


def kernel(x_prompt, x_sample, cache_k, cache_v, state_ret, norm_mix, w_in, lambda_q1, lambda_k1, lambda_q2, lambda_k2, attn_subln, ret_gn, w_branch_a, w_branch_r, w_out, norm_ffn, peer_wq, peer_keys, peer_u, peer_v, norm_final):
    assert w_in.shape[0] == 1, "single layer"
    wi = w_in[0]
    o_ret = 2 * A_QK + A_V
    o_gs = o_ret + 2 * R_QK + R_V
    o_ga = o_gs + R_V
    o_gr = o_ga + D_MODEL
    w = {
        "norm_mix": norm_mix[0],
        "w_attn": wi[:, :o_ret].astype(BF16),
        "w_ret": wi[:, o_ret:o_gs].astype(BF16),
        "w_gs": wi[:, o_gs:o_ga].astype(BF16),
        "w_ga": wi[:, o_ga:o_gr].astype(BF16),
        "w_gr": wi[:, o_gr:].astype(BF16),
        "lam": jnp.stack([lambda_q1[0], lambda_k1[0], lambda_q2[0], lambda_k2[0]]).astype(F32),
        "subln": attn_subln[0].reshape(1, A_VDIM),
        "ret_gn": ret_gn[0],
        "w_a": w_branch_a[0].astype(BF16),
        "w_r": w_branch_r[0].astype(BF16),
        "w_o": w_out[0].astype(BF16),
        "norm_ffn": norm_ffn[0],
        "w_q": peer_wq[0].astype(BF16),
        "keys": peer_keys[0].reshape(2 * PEER_HEADS, N_KEYS, PEER_HALF).astype(BF16),
        "u": peer_u[0].astype(BF16),
        "v": peer_v[0].astype(BF16),
        "norm_final": norm_final,
    }
    bd, past_len = cache_k.shape[1], cache_k.shape[2]
    past = (
        cache_k[0].reshape(bd, past_len, A_QK),
        cache_v[0].reshape(bd, past_len, A_V),
        state_ret[0],
        past_len,
    )
    y_p, k_p, v_p, s_p = _stream(x_prompt, w, None)
    y_s, k_s, v_s, s_s = _stream(x_sample, w, past)
    return (y_p, y_s, k_p, v_p, s_p, k_s, v_s, s_s)
```

```python
import functools
import math

import jax
import jax.numpy as jnp
from jax import lax
from jax.experimental import pallas as pl
from jax.experimental.pallas import tpu as pltpu

D_MODEL = 2048
CHUNK = 64
EPS = 1e-6
A_HEADS = 8
A_DIM = 64
A_VDIM = 2 * A_DIM
R_HEADS = 8
R_KDIM = 128
R_VDIM = 256
PEER_HEADS = 8
N_KEYS = 128
PEER_HALF = 128
PEER_TOPK = 16
A_QK = A_HEADS * 2 * A_DIM
A_V = A_HEADS * A_VDIM
R_QK = R_HEADS * R_KDIM
R_V = R_HEADS * R_VDIM
LAM_INIT = 0.8 - 0.6 * math.exp(-0.3 * 0)
ROPE_BASE = 10000.0
LOG2E = math.log2(math.e)

LANES = 128
MIB = 1024 * 1024

F32 = jnp.float32
BF16 = jnp.bfloat16


def _dot(a, b):
    return jnp.dot(a, b, preferred_element_type=F32)


def _dot_nt(a, b):
    return lax.dot_general(a, b, (((1,), (1,)), ((), ())), preferred_element_type=F32)


def _dot_tn(a, b):
    return lax.dot_general(a, b, (((0,), (0,)), ((), ())), preferred_element_type=F32)


def _sigmoid(x):
    return 1.0 / (1.0 + jnp.exp(-x))


def _params(semantics, vmem_mib):
    return pltpu.CompilerParams(dimension_semantics=semantics, vmem_limit_bytes=int(vmem_mib * MIB))


def _pick(n, pref):
    return pref if n % pref == 0 else n


def _resident(shape):
    nd = len(shape)
    return pl.BlockSpec(shape, lambda *_: (0,) * nd, pipeline_mode=pl.Buffered(1))


def _proj_attn_kernel(x_ref, g_ref, w_ref, h_ref, q_ref, kf_ref, vf_ref, kb_ref, vb_ref):
    x = x_ref[...]
    r = lax.rsqrt(jnp.mean(x * x, axis=-1, keepdims=True) + EPS)
    h = ((x * r) * g_ref[...]).astype(BF16)
    h_ref[...] = h
    q = _dot(h, w_ref[:, 0:A_QK])
    q_ref[...] = (q * (A_DIM ** -0.5 * LOG2E)).astype(BF16)
    k = _dot(h, w_ref[:, A_QK:2 * A_QK])
    kf_ref[...] = k
    kb_ref[...] = k.astype(BF16)
    v = _dot(h, w_ref[:, 2 * A_QK:2 * A_QK + A_V])
    vf_ref[...] = v
    vb_ref[...] = v.astype(BF16)


def _proj_attn(x, g, w):
    t = x.shape[0]
    tm = _pick(t, 512)
    row = lambda i: (i, 0)
    n = A_QK
    return pl.pallas_call(
        _proj_attn_kernel,
        out_shape=(
            jax.ShapeDtypeStruct((t, D_MODEL), BF16),
            jax.ShapeDtypeStruct((t, n), BF16),
            jax.ShapeDtypeStruct((t, n), F32),
            jax.ShapeDtypeStruct((t, n), F32),
            jax.ShapeDtypeStruct((t, n), BF16),
            jax.ShapeDtypeStruct((t, n), BF16),
        ),
        grid=(t // tm,),
        in_specs=[
            pl.BlockSpec((tm, D_MODEL), row),
            pl.BlockSpec((1, D_MODEL), lambda i: (0, 0)),
            _resident(w.shape),
        ],
        out_specs=(pl.BlockSpec((tm, D_MODEL), row),) + tuple(pl.BlockSpec((tm, n), row) for _ in range(5)),
        compiler_params=_params(("parallel",), 56),
        name="proj_attn",
    )(x, g.reshape(1, D_MODEL), w)


def _proj_ret_kernel(h_ref, w_ref, q_ref, k_ref, v_ref, *, tm, pos_period, pos_offset):
    i = pl.program_id(0)
    h = h_ref[...]
    half = R_KDIM // 2
    row = lax.broadcasted_iota(jnp.int32, (tm, R_KDIM), 0) + i * tm
    pos = (lax.rem(row, pos_period) + pos_offset).astype(F32)
    lane = lax.broadcasted_iota(jnp.int32, (tm, R_KDIM), 1)
    fidx = lax.rem(lane, half).astype(F32)
    freq = jnp.exp(fidx * (-math.log(ROPE_BASE) / half))
    ang = pos * freq
    cos = jnp.cos(ang)
    sin = jnp.sin(ang)
    sin_signed = jnp.where(lane < half, -sin, sin)

    def rotate_into(y, o_ref):
        for hh in range(R_HEADS):
            sl = slice(hh * R_KDIM, (hh + 1) * R_KDIM)
            yh = y[:, sl]
            o_ref[:, sl] = (yh * cos + pltpu.roll(yh, half, 1) * sin_signed).astype(BF16)

    rotate_into(_dot(h, w_ref[:, 0:R_QK]), q_ref)
    rotate_into(_dot(h, w_ref[:, R_QK:2 * R_QK]) * (R_KDIM ** -0.5), k_ref)
    v_ref[...] = _dot(h, w_ref[:, 2 * R_QK:2 * R_QK + R_V]).astype(BF16)


def _proj_ret(h, w, pos_period, pos_offset):
    t = h.shape[0]
    tm = _pick(t, 512)
    if pos_period < tm:
        assert tm % pos_period == 0
    else:
        assert pos_period % tm == 0
    row = lambda i: (i, 0)
    kern = functools.partial(_proj_ret_kernel, tm=tm, pos_period=pos_period, pos_offset=pos_offset)
    return pl.pallas_call(
        kern,
        out_shape=(
            jax.ShapeDtypeStruct((t, R_QK), BF16),
            jax.ShapeDtypeStruct((t, R_QK), BF16),
            jax.ShapeDtypeStruct((t, R_V), BF16),
        ),
        grid=(t // tm,),
        in_specs=[pl.BlockSpec((tm, D_MODEL), row), _resident(w.shape)],
        out_specs=(pl.BlockSpec((tm, R_QK), row), pl.BlockSpec((tm, R_QK), row), pl.BlockSpec((tm, R_V), row)),
        compiler_params=_params(("parallel",), 48),
        name="proj_ret",
    )(h, w)


def _lam_value(lam_ref):
    p = lam_ref[...]
    a = jnp.sum(p[0:1] * p[1:2], axis=-1, keepdims=True)
    b = jnp.sum(p[2:3] * p[3:4], axis=-1, keepdims=True)
    return jnp.exp(a) - jnp.exp(b) + LAM_INIT


def _split_maps(q):
    lane = lax.broadcasted_iota(jnp.int32, q.shape, 1)
    zero = jnp.zeros_like(q)
    return jnp.where(lane < A_DIM, q, zero), jnp.where(lane >= A_DIM, q, zero)


def _subln(o, subln_ref):
    r = lax.rsqrt(jnp.mean(o * o, axis=-1, keepdims=True) + EPS)
    return (((o * r) * subln_ref[...]) * (1.0 - LAM_INIT)).astype(BF16)


ATTN_ROWS = 32


def _attn_prompt_kernel(lam_ref, subln_ref, q_ref, k_ref, v_ref, o_ref,
                        s_scr, p_scr, m_scr, al_scr, acc_scr, *, tq):
    qi = pl.program_id(2)
    lam = _lam_value(lam_ref)
    q1, q2 = _split_maps(q_ref[0])
    m_scr[...] = jnp.full(m_scr.shape, -1e30, F32)
    acc_scr[...] = jnp.zeros(acc_scr.shape, F32)
    lane = lax.broadcasted_iota(jnp.int32, (ATTN_ROWS, LANES), 1)
    ones_col = (lax.broadcasted_iota(jnp.int32, (tq, LANES), 1) == 0).astype(BF16)

    def softmax_rows(sidx, mp, rc, masked, tie):
        rows = slice(rc * ATTN_ROWS, (rc + 1) * ATTN_ROWS)
        visible = ((rc * ATTN_ROWS) // CHUNK + 1) * CHUNK if masked else tq

        def load(c):
            lo = c * LANES
            s = s_scr[sidx, rows, lo:lo + LANES]
            if lo + LANES > visible:
                s = jnp.where(lane < visible - lo, s, -1e30)
            return s

        n_live = -(-visible // LANES)
        row_max = functools.reduce(jnp.maximum, [load(c) for c in range(n_live)])
        m_old = m_scr[mp, rows, :]
        m_new = jnp.maximum(m_old, jnp.max(row_max, axis=-1, keepdims=True))
        m_scr[mp, rows, :] = m_new
        al_scr[sidx, rows, :] = jnp.exp2(m_old - m_new)
        for c in range(tq // LANES):
            lo = c * LANES
            if c >= n_live:
                p_scr[sidx, rows, lo:lo + LANES] = jnp.zeros((ATTN_ROWS, LANES), BF16)
                continue
            p = jnp.exp2(load(c) - m_new)
            if tie is not None and c == n_live - 1:
                p = jnp.concatenate([p[0:8] + tie, p[8:]], axis=0)
            p_scr[sidx, rows, lo:lo + LANES] = p.astype(BF16)

    def scores(j, slot):
        k = k_ref[0, pl.ds(pl.multiple_of(j * tq, tq), tq), :]
        s_scr[2 * slot] = _dot_nt(q1, k)
        s_scr[2 * slot + 1] = _dot_nt(q2, k)

    n_rc = tq // ATTN_ROWS

    def attend(j, slot, masked, next_slot):
        start = pl.multiple_of(j * tq, tq)
        v_ext = jnp.concatenate([v_ref[0, pl.ds(start, tq), :], ones_col], axis=1)
        tie = None
        if next_slot is not None:
            tie = _ordered_zero(s_scr[2 * next_slot + 1, tq - 8:tq, tq - LANES:tq])
        for mp in range(2):
            for rc in range(n_rc):
                softmax_rows(2 * slot + mp, mp, rc, masked, tie if (mp == 1 and rc == n_rc - 1) else None)
            pv = _dot(p_scr[2 * slot + mp], v_ext)
            alpha = al_scr[2 * slot + mp]
            for half in range(2):
                cols = slice(half * A_VDIM, (half + 1) * A_VDIM)
                acc_scr[mp, :, cols] = alpha * acc_scr[mp, :, cols] + pv[:, cols]
            if mp == 0:
                first = _ordered_zero(pv[tq - 8:tq, 0:LANES])
                tie = first if tie is None else tie + first

    def pair_step(t, carry):
        scores(2 * t, 0)
        scores(2 * t + 1, 1)
        attend(2 * t, 0, False, 1)
        attend(2 * t + 1, 1, False, None)
        return carry

    lax.fori_loop(0, lax.div(qi, 2), pair_step, 0)

    @pl.when(lax.rem(qi, 2) == 0)
    def _():
        scores(qi, 0)
        attend(qi, 0, True, None)

    @pl.when(lax.rem(qi, 2) == 1)
    def _():
        scores(qi - 1, 0)
        scores(qi, 1)
        attend(qi - 1, 0, False, 1)
        attend(qi, 1, True, None)

    def normalised(mp):
        return acc_scr[mp, :, 0:A_VDIM] / acc_scr[mp, :, A_VDIM:A_VDIM + 1]

    o = normalised(0) - lam * normalised(1)
    o_ref[0] = _subln(o, subln_ref)


def _attn_prompt(q, k, v, lam_params, subln):
    b, s, _ = q.shape
    tq = _pick(s, 512)
    assert tq % CHUNK == 0 and tq % LANES == 0 and CHUNK % ATTN_ROWS == 0
    kern = functools.partial(_attn_prompt_kernel, tq=tq)
    return pl.pallas_call(
        kern,
        out_shape=jax.ShapeDtypeStruct((b, s, A_V), BF16),
        grid=(b, A_HEADS, s // tq),
        in_specs=[
            pl.BlockSpec((4, A_DIM), lambda bi, h, qi: (0, 0)),
            pl.BlockSpec((1, A_VDIM), lambda bi, h, qi: (0, 0)),
            pl.BlockSpec((1, tq, A_VDIM), lambda bi, h, qi: (bi, qi, h)),
            pl.BlockSpec((1, s, A_VDIM), lambda bi, h, qi: (bi, 0, h)),
            pl.BlockSpec((1, s, A_VDIM), lambda bi, h, qi: (bi, 0, h)),
        ],
        out_specs=pl.BlockSpec((1, tq, A_VDIM), lambda bi, h, qi: (bi, qi, h)),
        scratch_shapes=[
            pltpu.VMEM((4, tq, tq), F32),
            pltpu.VMEM((4, tq, tq), BF16),
            pltpu.VMEM((2, tq, LANES), F32),
            pltpu.VMEM((4, tq, LANES), F32),
            pltpu.VMEM((2, tq, 2 * A_VDIM), F32),
        ],
        compiler_params=_params(("parallel", "parallel", "arbitrary"), 48),
        name="diff_attn_prompt",
    )(lam_params, subln, q, k, v)


def _attn_sample_kernel(lam_ref, subln_ref, q_ref, kn_ref, vn_ref, ck_ref, cv_ref, o_ref):
    lam = _lam_value(lam_ref)
    for h in range(A_HEADS):
        sl = slice(h * A_VDIM, (h + 1) * A_VDIM)
        q1, q2 = _split_maps(q_ref[0, :, sl])
        kp = ck_ref[0, :, sl].astype(BF16)
        vp = cv_ref[0, :, sl].astype(BF16)
        kn = kn_ref[0, :, sl]
        vn = vn_ref[0, :, sl]

        def attend(qm):
            sp = _dot_nt(qm, kp)
            sn = _dot_nt(qm, kn)
            m = jnp.maximum(jnp.max(sp, axis=-1, keepdims=True), jnp.max(sn, axis=-1, keepdims=True))
            pp = jnp.exp2(sp - m)
            pn = jnp.exp2(sn - m)
            l = jnp.sum(pp, axis=-1, keepdims=True) + jnp.sum(pn, axis=-1, keepdims=True)
            return (_dot(pp.astype(BF16), vp) + _dot(pn.astype(BF16), vn)) / l

        o = attend(q1) - lam * attend(q2)
        o_ref[0, :, sl] = _subln(o, subln_ref)


def _attn_sample(q, kn, vn, cache_k, cache_v, lam_params, subln):
    b, l, _ = q.shape
    past = cache_k.shape[1]
    new = lambda bi: (bi, 0, 0)
    return pl.pallas_call(
        _attn_sample_kernel,
        out_shape=jax.ShapeDtypeStruct((b, l, A_V), BF16),
        grid=(b,),
        in_specs=[
            pl.BlockSpec((4, A_DIM), lambda bi: (0, 0)),
            pl.BlockSpec((1, A_VDIM), lambda bi: (0, 0)),
            pl.BlockSpec((1, l, A_QK), new),
            pl.BlockSpec((1, l, A_QK), new),
            pl.BlockSpec((1, l, A_V), new),
            pl.BlockSpec((1, past, A_QK), new),
            pl.BlockSpec((1, past, A_V), new),
        ],
        out_specs=pl.BlockSpec((1, l, A_V), new),
        compiler_params=_params(("parallel",), 48),
        name="diff_attn_sample",
    )(lam_params, subln, q, kn, vn, cache_k, cache_v)


def _retention_kernel(*refs, blk, has_init):
    if has_init:
        q_ref, k_ref, v_ref, gn_ref, s0_ref, o_ref, st_ref = refs
    else:
        q_ref, k_ref, v_ref, gn_ref, o_ref, st_ref = refs
    h = pl.program_id(1)
    step = pl.program_id(2)

    @pl.when(step == 0)
    def _():
        if has_init:
            st_ref[0, 0] = s0_ref[0, 0]
        else:
            st_ref[0, 0] = jnp.zeros((R_KDIM, R_VDIM), F32)

    hv = (jnp.zeros((1, 1), jnp.int32) + h).astype(F32)
    log_gamma = jnp.log(1.0 - jnp.exp(-(5.0 + hv) * math.log(2.0)))
    q = q_ref[0]
    k = k_ref[0]
    v = v_ref[0]
    ri = lax.broadcasted_iota(jnp.int32, (blk, blk), 0)
    ci = lax.broadcasted_iota(jnp.int32, (blk, blk), 1)
    decay = jnp.exp(log_gamma * jnp.abs(ri - ci).astype(F32))
    if blk > CHUNK:
        decay = jnp.where((ci // CHUNK) <= (ri // CHUNK), decay, 0.0)
    s = _dot_nt(q, k) * decay
    intra = _dot(s.astype(BF16), v)
    pos = lax.broadcasted_iota(jnp.int32, (blk, 1), 0).astype(F32)
    state = st_ref[0, 0]
    cross = _dot(q, state.astype(BF16)) * jnp.exp(log_gamma * (pos + 1.0))
    o = intra + cross
    r = lax.rsqrt(jnp.mean(o * o, axis=-1, keepdims=True) + EPS)
    o_ref[0] = ((o * r) * gn_ref[pl.ds(h, 1), :]).astype(BF16)
    k_decayed = (k.astype(F32) * jnp.exp(log_gamma * (blk - 1.0 - pos))).astype(BF16)
    st_ref[0, 0] = jnp.exp(log_gamma * float(blk)) * state + _dot_tn(k_decayed, v)


def _retention(q, k, v, gn, state0, blk):
    b, s, _ = q.shape
    has_init = state0 is not None
    kern = functools.partial(_retention_kernel, blk=blk, has_init=has_init)
    tok = lambda bi, h, j: (bi, j, h)
    st = lambda bi, h, j: (bi, h, 0, 0)
    in_specs = [
        pl.BlockSpec((1, blk, R_KDIM), tok),
        pl.BlockSpec((1, blk, R_KDIM), tok),
        pl.BlockSpec((1, blk, R_VDIM), tok),
        pl.BlockSpec((R_HEADS, R_VDIM), lambda bi, h, j: (0, 0)),
    ]
    args = [q, k, v, gn]
    if has_init:
        in_specs.append(pl.BlockSpec((1, 1, R_KDIM, R_VDIM), st))
        args.append(state0)
    return pl.pallas_call(
        kern,
        out_shape=(
            jax.ShapeDtypeStruct((b, s, R_V), BF16),
            jax.ShapeDtypeStruct((b, R_HEADS, R_KDIM, R_VDIM), F32),
        ),
        grid=(b, R_HEADS, s // blk),
        in_specs=in_specs,
        out_specs=(pl.BlockSpec((1, blk, R_VDIM), tok), pl.BlockSpec((1, 1, R_KDIM, R_VDIM), st)),
        compiler_params=_params(("parallel", "parallel", "arbitrary"), 32),
        name="retention_init" if has_init else "retention",
    )(*args)


def _swish_gate_kernel(h_ref, w_ref, orn_ref, o_ref):
    g = _dot(h_ref[...], w_ref[...])
    o_ref[...] = ((g * _sigmoid(g)) * orn_ref[...].astype(F32)).astype(BF16)


def _swish_gate(h, w, orn):
    t = h.shape[0]
    tm = _pick(t, 512)
    row = lambda i: (i, 0)
    return pl.pallas_call(
        _swish_gate_kernel,
        out_shape=jax.ShapeDtypeStruct((t, R_V), BF16),
        grid=(t // tm,),
        in_specs=[pl.BlockSpec((tm, D_MODEL), row), _resident(w.shape), pl.BlockSpec((tm, R_V), row)],
        out_specs=pl.BlockSpec((tm, R_V), row),
        compiler_params=_params(("parallel",), 40),
        name="swish_gate",
    )(h, w, orn)


def _merge_kernel(h_ref, oa_ref, orp_ref, wga_ref, wgr_ref, wa_ref, wr_ref, o_ref):
    h = h_ref[...]
    ga = _sigmoid(_dot(h, wga_ref[...]))
    gr = _sigmoid(_dot(h, wgr_ref[...]))
    o_ref[...] = (ga * _dot(oa_ref[...], wa_ref[...]) + gr * _dot(orp_ref[...], wr_ref[...])).astype(BF16)


def _merge(h, oa, orp, wga, wgr, wa, wr):
    t = h.shape[0]
    tm = _pick(t, 512)
    tn = 512
    row = lambda j, i: (i, 0)
    col = lambda j, i: (0, j)
    return pl.pallas_call(
        _merge_kernel,
        out_shape=jax.ShapeDtypeStruct((t, D_MODEL), BF16),
        grid=(D_MODEL // tn, t // tm),
        in_specs=[
            pl.BlockSpec((tm, D_MODEL), row),
            pl.BlockSpec((tm, A_V), row),
            pl.BlockSpec((tm, R_V), row),
            pl.BlockSpec((D_MODEL, tn), col),
            pl.BlockSpec((D_MODEL, tn), col),
            pl.BlockSpec((A_V, tn), col),
            pl.BlockSpec((R_V, tn), col),
        ],
        out_specs=pl.BlockSpec((tm, tn), lambda j, i: (i, j)),
        compiler_params=_params(("parallel", "parallel"), 40),
        name="merge_branches",
    )(h, oa, orp, wga, wgr, wa, wr)


def _out_proj_kernel(x_ref, m_ref, wo_ref, g_ref, wq_ref, x1_ref, h2_ref, qp_ref):
    x1 = x_ref[...] + _dot(m_ref[...], wo_ref[...])
    x1_ref[...] = x1
    r = lax.rsqrt(jnp.mean(x1 * x1, axis=-1, keepdims=True) + EPS)
    h2 = ((x1 * r) * g_ref[...]).astype(BF16)
    h2_ref[...] = h2
    qp_ref[...] = _dot(h2, wq_ref[...]).astype(BF16)


def _out_proj(x, m, wo, g, wq):
    t = x.shape[0]
    tm = _pick(t, 256)
    row = lambda i: (i, 0)
    return pl.pallas_call(
        _out_proj_kernel,
        out_shape=(
            jax.ShapeDtypeStruct((t, D_MODEL), F32),
            jax.ShapeDtypeStruct((t, D_MODEL), BF16),
            jax.ShapeDtypeStruct((t, D_MODEL), BF16),
        ),
        grid=(t // tm,),
        in_specs=[
            pl.BlockSpec((tm, D_MODEL), row),
            pl.BlockSpec((tm, D_MODEL), row),
            _resident(wo.shape),
            pl.BlockSpec((1, D_MODEL), lambda i: (0, 0)),
            _resident(wq.shape),
        ],
        out_specs=tuple(pl.BlockSpec((tm, D_MODEL), row) for _ in range(3)),
        compiler_params=_params(("parallel",), 48),
        name="out_proj",
    )(x, m, wo, g.reshape(1, D_MODEL), wq)


def _top_rows(x, n):
    rows = x.shape[0]
    ridx = lax.broadcasted_iota(jnp.int32, x.shape, 0)
    out = []
    for _ in range(n):
        m = jnp.max(x, axis=0, keepdims=True)
        out.append(m)
        first = jnp.min(jnp.where(x == m, ridx, rows), axis=0, keepdims=True)
        x = jnp.where(ridx == first, -jnp.inf, x)
    return out


_TOP_N = PEER_TOPK + 1
_TOP_ROWS = 24
_CAND_ROWS = _TOP_ROWS + 7 * 8 + (_TOP_ROWS - 8)


def _peer_route_kernel(qp_ref, keys_ref, s1_ref, s2_ref, thr_ref, top_scr, cand_scr):
    k = PEER_TOPK
    top_scr[:, _TOP_N:, :] = jnp.full((2, _TOP_ROWS - _TOP_N, top_scr.shape[2]), -jnp.inf, F32)
    for h in range(PEER_HEADS):
        for c in range(2):
            hc = 2 * h + c
            st = _dot_nt(keys_ref[hc], qp_ref[:, hc * PEER_HALF:(hc + 1) * PEER_HALF])
            (s2_ref if c else s1_ref)[h] = st
            for r, m in enumerate(_top_rows(st, _TOP_N)):
                top_scr[c, r:r + 1, :] = m
        v1 = top_scr[0]
        v2 = top_scr[1]
        cand_scr[0:_TOP_ROWS, :] = v1[0:1] + v2
        for a in range(1, 8):
            cand_scr[_TOP_ROWS + 8 * (a - 1):_TOP_ROWS + 8 * a, :] = v1[a:a + 1] + v2[0:8]
        cand_scr[_TOP_ROWS + 56:, :] = v1[8:_TOP_ROWS] + v2[0:1]
        best = _top_rows(cand_scr[...], _TOP_N)
        z = jnp.zeros_like(best[0])
        for m in best[:k]:
            z = z + jnp.exp(m - best[0])
        off = best[0] + jnp.log(z)
        thr_ref[h:h + 1, :] = (0.5 * (best[k - 1] + best[k]) - off) * LOG2E - 1.0
        s1_ref[h] = (s1_ref[h] - off) * LOG2E - 1.0
        s2_ref[h] = s2_ref[h] * LOG2E


def _peer_route(qp, keys):
    t = qp.shape[0]
    tm = _pick(t, 512)
    return pl.pallas_call(
        _peer_route_kernel,
        out_shape=(
            jax.ShapeDtypeStruct((PEER_HEADS, N_KEYS, t), F32),
            jax.ShapeDtypeStruct((PEER_HEADS, N_KEYS, t), F32),
            jax.ShapeDtypeStruct((PEER_HEADS, t), F32),
        ),
        grid=(t // tm,),
        in_specs=[
            pl.BlockSpec((tm, D_MODEL), lambda i: (i, 0)),
            pl.BlockSpec(keys.shape, lambda i: (0, 0, 0)),
        ],
        out_specs=(
            pl.BlockSpec((PEER_HEADS, N_KEYS, tm), lambda i: (0, 0, i)),
            pl.BlockSpec((PEER_HEADS, N_KEYS, tm), lambda i: (0, 0, i)),
            pl.BlockSpec((PEER_HEADS, tm), lambda i: (0, i)),
        ),
        scratch_shapes=[pltpu.VMEM((2, _TOP_ROWS, tm), F32), pltpu.VMEM((_CAND_ROWS, tm), F32)],
        compiler_params=_params(("parallel",), 40),
        name="peer_route",
    )(qp, keys)


GATE_ROWS = 16


def _ordered_zero(x):
    bits = lax.bitcast_convert_type(x, jnp.int32)
    return lax.shift_right_logical(lax.shift_right_logical(bits, 16), 16).astype(F32)


def _peer_expert_kernel(h2_ref, u_ref, v_ref, s1_first_ref, s1_next_ref, s2_ref, thr_ref, x1_ref, gf_ref,
                        o_ref, w_scr, act_scr, *, tm, te, n_tiles):
    j = pl.program_id(1)
    n_key1 = te // N_KEYS

    def build_gates(s1_ref, slot, key1_rows):
        link = jnp.zeros((8, LANES), F32)
        for a in key1_rows:
            for g in range(tm // LANES):
                ls = slice(g * LANES, (g + 1) * LANES)
                for r in range(N_KEYS // GATE_ROWS):
                    k2 = slice(r * GATE_ROWS, (r + 1) * GATE_ROWS)
                    w = jnp.concatenate([link] * (GATE_ROWS // 8), axis=0)
                    for h in range(PEER_HEADS):
                        t = s1_ref[h, a:a + 1, ls] + s2_ref[h, k2, ls]
                        w = w + jnp.where(t >= thr_ref[h:h + 1, ls], jnp.exp2(t), 0.0)
                    w_scr[slot, a * N_KEYS + r * GATE_ROWS:a * N_KEYS + (r + 1) * GATE_ROWS, ls] = w
                    link = _ordered_zero(w[0:8])
        return link

    def run_experts(slot, tie):
        sc = _dot_nt(u_ref[...], h2_ref[...])
        for a in range(n_key1):
            for g in range(tm // LANES):
                ls = slice(g * LANES, (g + 1) * LANES)
                rows = slice(a * N_KEYS, (a + 1) * N_KEYS)
                x = sc[rows, ls]
                act = (x * (1.0 + lax.erf(x * math.sqrt(0.5)))) * w_scr[slot, rows, ls]
                if tie is not None and a == n_key1 - 1 and g == tm // LANES - 1:
                    act = jnp.concatenate([act[0:8] + tie, act[8:]], axis=0)
                act_scr[rows, ls] = act.astype(BF16)
        o_ref[...] += _dot_tn(act_scr[...], v_ref[...])

    @pl.when(j == 0)
    def _():
        o_ref[...] = jnp.zeros_like(o_ref)
        build_gates(s1_first_ref, 0, range(n_key1))

    for parity in range(2):
        @pl.when((lax.rem(j, 2) == parity) & (j < n_tiles - 1))
        def _():
            tie = build_gates(s1_next_ref, 1 - parity, range(0, n_key1 // 2))
            run_experts(parity, tie)
            build_gates(s1_next_ref, 1 - parity, range(n_key1 // 2, n_key1))

        if (n_tiles - 1) % 2 == parity:
            @pl.when(j == n_tiles - 1)
            def _():
                run_experts(parity, None)
                x = x1_ref[...] + o_ref[...]
                r = lax.rsqrt(jnp.mean(x * x, axis=-1, keepdims=True) + EPS)
                o_ref[...] = (x * r) * gf_ref[...]


def _peer_experts(h2, u, v, s1, s2, thr, x1, g_final):
    t = h2.shape[0]
    ne = u.shape[0]
    tm = _pick(t, 512)
    te = 1024
    n_tiles = ne // te
    n_key1 = te // N_KEYS
    assert n_tiles >= 2 and n_key1 % 8 == 0
    kern = functools.partial(_peer_expert_kernel, tm=tm, te=te, n_tiles=n_tiles)
    return pl.pallas_call(
        kern,
        out_shape=jax.ShapeDtypeStruct((t, D_MODEL), F32),
        grid=(t // tm, n_tiles),
        in_specs=[
            pl.BlockSpec((tm, D_MODEL), lambda i, j: (i, 0)),
            pl.BlockSpec((te, D_MODEL), lambda i, j: (j, 0)),
            pl.BlockSpec((te, D_MODEL), lambda i, j: (j, 0)),
            pl.BlockSpec((PEER_HEADS, n_key1, tm), lambda i, j: (0, 0, i)),
            pl.BlockSpec((PEER_HEADS, n_key1, tm), lambda i, j: (0, jnp.minimum(j + 1, n_tiles - 1), i)),
            pl.BlockSpec((PEER_HEADS, N_KEYS, tm), lambda i, j: (0, 0, i)),
            pl.BlockSpec((PEER_HEADS, tm), lambda i, j: (0, i)),
            pl.BlockSpec((tm, D_MODEL), lambda i, j: (i, 0)),
            pl.BlockSpec((1, D_MODEL), lambda i, j: (0, 0)),
        ],
        out_specs=pl.BlockSpec((tm, D_MODEL), lambda i, j: (i, 0)),
        scratch_shapes=[pltpu.VMEM((2, te, tm), F32), pltpu.VMEM((te, tm), BF16)],
        compiler_params=_params(("parallel", "arbitrary"), 58),
        name="peer_experts",
    )(h2, u, v, s1, s1, s2, thr, x1, g_final.reshape(1, D_MODEL))


def _stream(x, w, past):
    b, l, d = x.shape
    t = b * l
    x2 = x.reshape(t, d)
    h, qa, kf, vf, kb, vb = _proj_attn(x2, w["norm_mix"], w["w_attn"])
    if past is None:
        qr, kr, vr = _proj_ret(h, w["w_ret"], l, 0)
    else:
        qr, kr, vr = _proj_ret(h, w["w_ret"], l, past[3])
    as3 = lambda a: a.reshape(b, l, a.shape[-1])
    if past is None:
        oa = _attn_prompt(as3(qa), as3(kb), as3(vb), w["lam"], w["subln"])
        blk = _pick(l, 512)
        orn, state = _retention(as3(qr), as3(kr), as3(vr), w["ret_gn"], None, blk)
    else:
        cache_k, cache_v, state0, _ = past
        oa = _attn_sample(as3(qa), as3(kb), as3(vb), cache_k, cache_v, w["lam"], w["subln"])
        orn, state = _retention(as3(qr), as3(kr), as3(vr), w["ret_gn"], state0, l)
    orp = _swish_gate(h, w["w_gs"], orn.reshape(t, R_V))
    m = _merge(h, oa.reshape(t, A_V), orp, w["w_ga"], w["w_gr"], w["w_a"], w["w_r"])
    x1, h2, qp = _out_proj(x2, m, w["w_o"], w["norm_ffn"], w["w_q"])
    s1, s2, thr = _peer_route(qp, w["keys"])
    y = _peer_experts(h2, w["u"], w["v"], s1, s2, thr, x1, w["norm_final"])
    k_rows = kf.reshape(1, b, l, A_HEADS, 2 * A_DIM)
    v_rows = vf.reshape(1, b, l, A_HEADS, A_VDIM)
    return y.reshape(b, l, d), k_rows, v_rows, state[None]


def kernel(x_prompt, x_sample, cache_k, cache_v, state_ret, norm_mix, w_in, lambda_q1, lambda_k1, lambda_q2, lambda_k2, attn_subln, ret_gn, w_branch_a, w_branch_r, w_out, norm_ffn, peer_wq, peer_keys, peer_u, peer_v, norm_final):
    assert w_in.shape[0] == 1, "single layer"
    wi = w_in[0]
    o_ret = 2 * A_QK + A_V
    o_gs = o_ret + 2 * R_QK + R_V
    o_ga = o_gs + R_V
    o_gr = o_ga + D_MODEL
    w = {
        "norm_mix": norm_mix[0],
        "w_attn": wi[:, :o_ret].astype(BF16),
        "w_ret": wi[:, o_ret:o_gs].astype(BF16),
        "w_gs": wi[:, o_gs:o_ga].astype(BF16),
        "w_ga": wi[:, o_ga:o_gr].astype(BF16),
        "w_gr": wi[:, o_gr:].astype(BF16),
        "lam": jnp.stack([lambda_q1[0], lambda_k1[0], lambda_q2[0], lambda_k2[0]]).astype(F32),
        "subln": attn_subln[0].reshape(1, A_VDIM),
        "ret_gn": ret_gn[0],
        "w_a": w_branch_a[0].astype(BF16),
        "w_r": w_branch_r[0].astype(BF16),
        "w_o": w_out[0].astype(BF16),
        "norm_ffn": norm_ffn[0],
        "w_q": peer_wq[0].astype(BF16),
        "keys": peer_keys[0].reshape(2 * PEER_HEADS, N_KEYS, PEER_HALF).astype(BF16),
        "u": peer_u[0].astype(BF16),
        "v": peer_v[0].astype(BF16),
        "norm_final": norm_final,
    }
    bd, past_len = cache_k.shape[1], cache_k.shape[2]
    past = (
        cache_k[0].reshape(bd, past_len, A_QK),
        cache_v[0].reshape(bd, past_len, A_V),
        state_ret[0],
        past_len,
    )
    y_p, k_p, v_p, s_p = _stream(x_prompt, w, None)
    y_s, k_s, v_s, s_s = _stream(x_sample, w, past)
    return (y_p, y_s, k_p, v_p, s_p, k_s, v_s, s_s)
```

```python
import functools
import math

import jax
import jax.numpy as jnp
from jax import lax
from jax.experimental import pallas as pl
from jax.experimental.pallas import tpu as pltpu

D_MODEL = 2048
CHUNK = 64
EPS = 1e-6
A_HEADS = 8
A_DIM = 64
A_VDIM = 2 * A_DIM
R_HEADS = 8
R_KDIM = 128
R_VDIM = 256
PEER_HEADS = 8
N_KEYS = 128
PEER_HALF = 128
PEER_TOPK = 16
A_QK = A_HEADS * 2 * A_DIM
A_V = A_HEADS * A_VDIM
R_QK = R_HEADS * R_KDIM
R_V = R_HEADS * R_VDIM
LAM_INIT = 0.8 - 0.6 * math.exp(-0.3 * 0)
ROPE_BASE = 10000.0
LOG2E = math.log2(math.e)

LANES = 128
MIB = 1024 * 1024

F32 = jnp.float32
BF16 = jnp.bfloat16


def _dot(a, b):
    return jnp.dot(a, b, preferred_element_type=F32)


def _dot_nt(a, b):
    return lax.dot_general(a, b, (((1,), (1,)), ((), ())), preferred_element_type=F32)


def _dot_tn(a, b):
    return lax.dot_general(a, b, (((0,), (0,)), ((), ())), preferred_element_type=F32)


def _sigmoid(x):
    return 1.0 / (1.0 + jnp.exp(-x))


def _params(semantics, vmem_mib):
    return pltpu.CompilerParams(dimension_semantics=semantics, vmem_limit_bytes=int(vmem_mib * MIB))


def _pick(n, pref):
    return pref if n % pref == 0 else n


def _resident(shape):
    nd = len(shape)
    return pl.BlockSpec(shape, lambda *_: (0,) * nd, pipeline_mode=pl.Buffered(1))


def _proj_attn_kernel(x_ref, g_ref, w_ref, h_ref, q_ref, kf_ref, vf_ref, kb_ref, vb_ref):
    x = x_ref[...]
    r = lax.rsqrt(jnp.mean(x * x, axis=-1, keepdims=True) + EPS)
    h = ((x * r) * g_ref[...]).astype(BF16)
    h_ref[...] = h
    q = _dot(h, w_ref[:, 0:A_QK])
    q_ref[...] = (q * (A_DIM ** -0.5 * LOG2E)).astype(BF16)
    k = _dot(h, w_ref[:, A_QK:2 * A_QK])
    kf_ref[...] = k
    kb_ref[...] = k.astype(BF16)
    v = _dot(h, w_ref[:, 2 * A_QK:2 * A_QK + A_V])
    vf_ref[...] = v
    vb_ref[...] = v.astype(BF16)


def _proj_attn(x, g, w):
    t = x.shape[0]
    tm = _pick(t, 512)
    row = lambda i: (i, 0)
    n = A_QK
    return pl.pallas_call(
        _proj_attn_kernel,
        out_shape=(
            jax.ShapeDtypeStruct((t, D_MODEL), BF16),
            jax.ShapeDtypeStruct((t, n), BF16),
            jax.ShapeDtypeStruct((t, n), F32),
            jax.ShapeDtypeStruct((t, n), F32),
            jax.ShapeDtypeStruct((t, n), BF16),
            jax.ShapeDtypeStruct((t, n), BF16),
        ),
        grid=(t // tm,),
        in_specs=[
            pl.BlockSpec((tm, D_MODEL), row),
            pl.BlockSpec((1, D_MODEL), lambda i: (0, 0)),
            _resident(w.shape),
        ],
        out_specs=(pl.BlockSpec((tm, D_MODEL), row),) + tuple(pl.BlockSpec((tm, n), row) for _ in range(5)),
        compiler_params=_params(("parallel",), 56),
        name="proj_attn",
    )(x, g.reshape(1, D_MODEL), w)


def _proj_ret_kernel(h_ref, w_ref, q_ref, k_ref, v_ref, *, tm, pos_period, pos_offset):
    i = pl.program_id(0)
    h = h_ref[...]
    half = R_KDIM // 2
    row = lax.broadcasted_iota(jnp.int32, (tm, R_KDIM), 0) + i * tm
    pos = (lax.rem(row, pos_period) + pos_offset).astype(F32)
    lane = lax.broadcasted_iota(jnp.int32, (tm, R_KDIM), 1)
    fidx = lax.rem(lane, half).astype(F32)
    freq = jnp.exp(fidx * (-math.log(ROPE_BASE) / half))
    ang = pos * freq
    cos = jnp.cos(ang)
    sin = jnp.sin(ang)
    sin_signed = jnp.where(lane < half, -sin, sin)

    def rotate_into(y, o_ref):
        for hh in range(R_HEADS):
            sl = slice(hh * R_KDIM, (hh + 1) * R_KDIM)
            yh = y[:, sl]
            o_ref[:, sl] = (yh * cos + pltpu.roll(yh, half, 1) * sin_signed).astype(BF16)

    rotate_into(_dot(h, w_ref[:, 0:R_QK]), q_ref)
    rotate_into(_dot(h, w_ref[:, R_QK:2 * R_QK]) * (R_KDIM ** -0.5), k_ref)
    v_ref[...] = _dot(h, w_ref[:, 2 * R_QK:2 * R_QK + R_V]).astype(BF16)


def _proj_ret(h, w, pos_period, pos_offset):
    t = h.shape[0]
    tm = _pick(t, 512)
    if pos_period < tm:
        assert tm % pos_period == 0
    else:
        assert pos_period % tm == 0
    row = lambda i: (i, 0)
    kern = functools.partial(_proj_ret_kernel, tm=tm, pos_period=pos_period, pos_offset=pos_offset)
    return pl.pallas_call(
        kern,
        out_shape=(
            jax.ShapeDtypeStruct((t, R_QK), BF16),
            jax.ShapeDtypeStruct((t, R_QK), BF16),
            jax.ShapeDtypeStruct((t, R_V), BF16),
        ),
        grid=(t // tm,),
        in_specs=[pl.BlockSpec((tm, D_MODEL), row), _resident(w.shape)],
        out_specs=(pl.BlockSpec((tm, R_QK), row), pl.BlockSpec((tm, R_QK), row), pl.BlockSpec((tm, R_V), row)),
        compiler_params=_params(("parallel",), 48),
        name="proj_ret",
    )(h, w)


def _lam_value(lam_ref):
    p = lam_ref[...]
    a = jnp.sum(p[0:1] * p[1:2], axis=-1, keepdims=True)
    b = jnp.sum(p[2:3] * p[3:4], axis=-1, keepdims=True)
    return jnp.exp(a) - jnp.exp(b) + LAM_INIT


def _split_maps(q):
    lane = lax.broadcasted_iota(jnp.int32, q.shape, 1)
    zero = jnp.zeros_like(q)
    return jnp.where(lane < A_DIM, q, zero), jnp.where(lane >= A_DIM, q, zero)


def _subln(o, subln_ref):
    r = lax.rsqrt(jnp.mean(o * o, axis=-1, keepdims=True) + EPS)
    return (((o * r) * subln_ref[...]) * (1.0 - LAM_INIT)).astype(BF16)


ATTN_ROWS = 32
ATTN_BLOCKS = 3


def _attn_prompt_kernel(lam_ref, subln_ref, q_ref, k_ref, v_ref, o_ref,
                        s_scr, p_scr, m_scr, al_scr, acc_scr, *, tq):
    qi = pl.program_id(2)
    lam = _lam_value(lam_ref)
    q1, q2 = _split_maps(q_ref[0])
    m_scr[...] = jnp.full(m_scr.shape, -1e30, F32)
    acc_scr[...] = jnp.zeros(acc_scr.shape, F32)
    lane = lax.broadcasted_iota(jnp.int32, (ATTN_ROWS, LANES), 1)
    ones_col = (lax.broadcasted_iota(jnp.int32, (tq, LANES), 1) == 0).astype(BF16)

    def softmax_rows(sidx, mp, rc, masked, tie):
        rows = slice(rc * ATTN_ROWS, (rc + 1) * ATTN_ROWS)
        visible = ((rc * ATTN_ROWS) // CHUNK + 1) * CHUNK if masked else tq

        def load(c):
            lo = c * LANES
            s = s_scr[sidx, rows, lo:lo + LANES]
            if lo + LANES > visible:
                s = jnp.where(lane < visible - lo, s, -1e30)
            return s

        n_live = -(-visible // LANES)
        row_max = functools.reduce(jnp.maximum, [load(c) for c in range(n_live)])
        m_old = m_scr[mp, rows, :]
        m_new = jnp.maximum(m_old, jnp.max(row_max, axis=-1, keepdims=True))
        m_scr[mp, rows, :] = m_new
        al_scr[sidx, rows, :] = jnp.exp2(m_old - m_new)
        for c in range(tq // LANES):
            lo = c * LANES
            if c >= n_live:
                p_scr[sidx, rows, lo:lo + LANES] = jnp.zeros((ATTN_ROWS, LANES), BF16)
                continue
            p = jnp.exp2(load(c) - m_new)
            if tie is not None and c == n_live - 1:
                p = jnp.concatenate([p[0:8] + tie, p[8:]], axis=0)
            p_scr[sidx, rows, lo:lo + LANES] = p.astype(BF16)

    def scores(j, slot):
        k = k_ref[0, pl.ds(pl.multiple_of(j * tq, tq), tq), :]
        s_scr[2 * slot] = _dot_nt(q1, k)
        s_scr[2 * slot + 1] = _dot_nt(q2, k)

    n_rc = tq // ATTN_ROWS

    def attend(j, slot, masked, next_slot):
        start = pl.multiple_of(j * tq, tq)
        v_ext = jnp.concatenate([v_ref[0, pl.ds(start, tq), :], ones_col], axis=1)
        tie = None
        if next_slot is not None:
            tie = _ordered_zero(s_scr[2 * next_slot + 1, tq - 8:tq, tq - LANES:tq])
        for mp in range(2):
            for rc in range(n_rc):
                softmax_rows(2 * slot + mp, mp, rc, masked, tie if (mp == 1 and rc == n_rc - 1) else None)
            pv = _dot(p_scr[2 * slot + mp], v_ext)
            alpha = al_scr[2 * slot + mp]
            for half in range(2):
                cols = slice(half * A_VDIM, (half + 1) * A_VDIM)
                acc_scr[mp, :, cols] = alpha * acc_scr[mp, :, cols] + pv[:, cols]
            if mp == 0:
                first = _ordered_zero(pv[tq - 8:tq, 0:LANES])
                tie = first if tie is None else tie + first

    def trip(first, n_blocks, last_masked):
        for i in range(n_blocks):
            scores(first + i, i)
        for i in range(n_blocks):
            attend(first + i, i, last_masked and i == n_blocks - 1, i + 1 if i + 1 < n_blocks else None)

    def loop_step(t, carry):
        trip(ATTN_BLOCKS * t, ATTN_BLOCKS, False)
        return carry

    lax.fori_loop(0, lax.div(qi, ATTN_BLOCKS), loop_step, 0)
    for r in range(ATTN_BLOCKS):
        @pl.when(lax.rem(qi, ATTN_BLOCKS) == r)
        def _():
            trip(qi - r, r + 1, True)

    def normalised(mp):
        return acc_scr[mp, :, 0:A_VDIM] / acc_scr[mp, :, A_VDIM:A_VDIM + 1]

    o = normalised(0) - lam * normalised(1)
    o_ref[0] = _subln(o, subln_ref)


def _attn_prompt(q, k, v, lam_params, subln):
    b, s, _ = q.shape
    tq = _pick(s, 512)
    assert tq % CHUNK == 0 and tq % LANES == 0 and CHUNK % ATTN_ROWS == 0
    kern = functools.partial(_attn_prompt_kernel, tq=tq)
    return pl.pallas_call(
        kern,
        out_shape=jax.ShapeDtypeStruct((b, s, A_V), BF16),
        grid=(b, A_HEADS, s // tq),
        in_specs=[
            pl.BlockSpec((4, A_DIM), lambda bi, h, qi: (0, 0)),
            pl.BlockSpec((1, A_VDIM), lambda bi, h, qi: (0, 0)),
            pl.BlockSpec((1, tq, A_VDIM), lambda bi, h, qi: (bi, qi, h)),
            pl.BlockSpec((1, s, A_VDIM), lambda bi, h, qi: (bi, 0, h)),
            pl.BlockSpec((1, s, A_VDIM), lambda bi, h, qi: (bi, 0, h)),
        ],
        out_specs=pl.BlockSpec((1, tq, A_VDIM), lambda bi, h, qi: (bi, qi, h)),
        scratch_shapes=[
            pltpu.VMEM((2 * ATTN_BLOCKS, tq, tq), F32),
            pltpu.VMEM((2 * ATTN_BLOCKS, tq, tq), BF16),
            pltpu.VMEM((2, tq, LANES), F32),
            pltpu.VMEM((2 * ATTN_BLOCKS, tq, LANES), F32),
            pltpu.VMEM((2, tq, 2 * A_VDIM), F32),
        ],
        compiler_params=_params(("parallel", "parallel", "arbitrary"), 48),
        name="diff_attn_prompt",
    )(lam_params, subln, q, k, v)


def _attn_sample_kernel(lam_ref, subln_ref, q_ref, kn_ref, vn_ref, ck_ref, cv_ref, o_ref):
    lam = _lam_value(lam_ref)
    for h in range(A_HEADS):
        sl = slice(h * A_VDIM, (h + 1) * A_VDIM)
        q1, q2 = _split_maps(q_ref[0, :, sl])
        kp = ck_ref[0, :, sl].astype(BF16)
        vp = cv_ref[0, :, sl].astype(BF16)
        kn = kn_ref[0, :, sl]
        vn = vn_ref[0, :, sl]

        def attend(qm):
            sp = _dot_nt(qm, kp)
            sn = _dot_nt(qm, kn)
            m = jnp.maximum(jnp.max(sp, axis=-1, keepdims=True), jnp.max(sn, axis=-1, keepdims=True))
            pp = jnp.exp2(sp - m)
            pn = jnp.exp2(sn - m)
            l = jnp.sum(pp, axis=-1, keepdims=True) + jnp.sum(pn, axis=-1, keepdims=True)
            return (_dot(pp.astype(BF16), vp) + _dot(pn.astype(BF16), vn)) / l

        o = attend(q1) - lam * attend(q2)
        o_ref[0, :, sl] = _subln(o, subln_ref)


def _attn_sample(q, kn, vn, cache_k, cache_v, lam_params, subln):
    b, l, _ = q.shape
    past = cache_k.shape[1]
    new = lambda bi: (bi, 0, 0)
    return pl.pallas_call(
        _attn_sample_kernel,
        out_shape=jax.ShapeDtypeStruct((b, l, A_V), BF16),
        grid=(b,),
        in_specs=[
            pl.BlockSpec((4, A_DIM), lambda bi: (0, 0)),
            pl.BlockSpec((1, A_VDIM), lambda bi: (0, 0)),
            pl.BlockSpec((1, l, A_QK), new),
            pl.BlockSpec((1, l, A_QK), new),
            pl.BlockSpec((1, l, A_V), new),
            pl.BlockSpec((1, past, A_QK), new),
            pl.BlockSpec((1, past, A_V), new),
        ],
        out_specs=pl.BlockSpec((1, l, A_V), new),
        compiler_params=_params(("parallel",), 48),
        name="diff_attn_sample",
    )(lam_params, subln, q, kn, vn, cache_k, cache_v)


def _retention_kernel(*refs, blk, has_init):
    if has_init:
        q_ref, k_ref, v_ref, gn_ref, s0_ref, o_ref, st_ref = refs
    else:
        q_ref, k_ref, v_ref, gn_ref, o_ref, st_ref = refs
    h = pl.program_id(1)
    step = pl.program_id(2)

    @pl.when(step == 0)
    def _():
        if has_init:
            st_ref[0, 0] = s0_ref[0, 0]
        else:
            st_ref[0, 0] = jnp.zeros((R_KDIM, R_VDIM), F32)

    hv = (jnp.zeros((1, 1), jnp.int32) + h).astype(F32)
    log_gamma = jnp.log(1.0 - jnp.exp(-(5.0 + hv) * math.log(2.0)))
    q = q_ref[0]
    k = k_ref[0]
    v = v_ref[0]
    ri = lax.broadcasted_iota(jnp.int32, (blk, blk), 0)
    ci = lax.broadcasted_iota(jnp.int32, (blk, blk), 1)
    decay = jnp.exp(log_gamma * jnp.abs(ri - ci).astype(F32))
    if blk > CHUNK:
        decay = jnp.where((ci // CHUNK) <= (ri // CHUNK), decay, 0.0)
    s = _dot_nt(q, k) * decay
    intra = _dot(s.astype(BF16), v)
    pos = lax.broadcasted_iota(jnp.int32, (blk, 1), 0).astype(F32)
    state = st_ref[0, 0]
    cross = _dot(q, state.astype(BF16)) * jnp.exp(log_gamma * (pos + 1.0))
    o = intra + cross
    r = lax.rsqrt(jnp.mean(o * o, axis=-1, keepdims=True) + EPS)
    o_ref[0] = ((o * r) * gn_ref[pl.ds(h, 1), :]).astype(BF16)
    k_decayed = (k.astype(F32) * jnp.exp(log_gamma * (blk - 1.0 - pos))).astype(BF16)
    st_ref[0, 0] = jnp.exp(log_gamma * float(blk)) * state + _dot_tn(k_decayed, v)


def _retention(q, k, v, gn, state0, blk):
    b, s, _ = q.shape
    has_init = state0 is not None
    kern = functools.partial(_retention_kernel, blk=blk, has_init=has_init)
    tok = lambda bi, h, j: (bi, j, h)
    st = lambda bi, h, j: (bi, h, 0, 0)
    in_specs = [
        pl.BlockSpec((1, blk, R_KDIM), tok),
        pl.BlockSpec((1, blk, R_KDIM), tok),
        pl.BlockSpec((1, blk, R_VDIM), tok),
        pl.BlockSpec((R_HEADS, R_VDIM), lambda bi, h, j: (0, 0)),
    ]
    args = [q, k, v, gn]
    if has_init:
        in_specs.append(pl.BlockSpec((1, 1, R_KDIM, R_VDIM), st))
        args.append(state0)
    return pl.pallas_call(
        kern,
        out_shape=(
            jax.ShapeDtypeStruct((b, s, R_V), BF16),
            jax.ShapeDtypeStruct((b, R_HEADS, R_KDIM, R_VDIM), F32),
        ),
        grid=(b, R_HEADS, s // blk),
        in_specs=in_specs,
        out_specs=(pl.BlockSpec((1, blk, R_VDIM), tok), pl.BlockSpec((1, 1, R_KDIM, R_VDIM), st)),
        compiler_params=_params(("parallel", "parallel", "arbitrary"), 32),
        name="retention_init" if has_init else "retention",
    )(*args)


def _swish_gate_kernel(h_ref, w_ref, orn_ref, o_ref):
    g = _dot(h_ref[...], w_ref[...])
    o_ref[...] = ((g * _sigmoid(g)) * orn_ref[...].astype(F32)).astype(BF16)


def _swish_gate(h, w, orn):
    t = h.shape[0]
    tm = _pick(t, 512)
    row = lambda i: (i, 0)
    return pl.pallas_call(
        _swish_gate_kernel,
        out_shape=jax.ShapeDtypeStruct((t, R_V), BF16),
        grid=(t // tm,),
        in_specs=[pl.BlockSpec((tm, D_MODEL), row), _resident(w.shape), pl.BlockSpec((tm, R_V), row)],
        out_specs=pl.BlockSpec((tm, R_V), row),
        compiler_params=_params(("parallel",), 40),
        name="swish_gate",
    )(h, w, orn)


def _merge_kernel(h_ref, oa_ref, orp_ref, wga_ref, wgr_ref, wa_ref, wr_ref, o_ref):
    h = h_ref[...]
    ga = _sigmoid(_dot(h, wga_ref[...]))
    gr = _sigmoid(_dot(h, wgr_ref[...]))
    o_ref[...] = (ga * _dot(oa_ref[...], wa_ref[...]) + gr * _dot(orp_ref[...], wr_ref[...])).astype(BF16)


def _merge(h, oa, orp, wga, wgr, wa, wr):
    t = h.shape[0]
    tm = _pick(t, 512)
    tn = 512
    row = lambda j, i: (i, 0)
    col = lambda j, i: (0, j)
    return pl.pallas_call(
        _merge_kernel,
        out_shape=jax.ShapeDtypeStruct((t, D_MODEL), BF16),
        grid=(D_MODEL // tn, t // tm),
        in_specs=[
            pl.BlockSpec((tm, D_MODEL), row),
            pl.BlockSpec((tm, A_V), row),
            pl.BlockSpec((tm, R_V), row),
            pl.BlockSpec((D_MODEL, tn), col),
            pl.BlockSpec((D_MODEL, tn), col),
            pl.BlockSpec((A_V, tn), col),
            pl.BlockSpec((R_V, tn), col),
        ],
        out_specs=pl.BlockSpec((tm, tn), lambda j, i: (i, j)),
        compiler_params=_params(("parallel", "parallel"), 40),
        name="merge_branches",
    )(h, oa, orp, wga, wgr, wa, wr)


def _out_proj_kernel(x_ref, m_ref, wo_ref, g_ref, wq_ref, x1_ref, h2_ref, qp_ref):
    x1 = x_ref[...] + _dot(m_ref[...], wo_ref[...])
    x1_ref[...] = x1
    r = lax.rsqrt(jnp.mean(x1 * x1, axis=-1, keepdims=True) + EPS)
    h2 = ((x1 * r) * g_ref[...]).astype(BF16)
    h2_ref[...] = h2
    qp_ref[...] = _dot(h2, wq_ref[...]).astype(BF16)


def _out_proj(x, m, wo, g, wq):
    t = x.shape[0]
    tm = _pick(t, 256)
    row = lambda i: (i, 0)
    return pl.pallas_call(
        _out_proj_kernel,
        out_shape=(
            jax.ShapeDtypeStruct((t, D_MODEL), F32),
            jax.ShapeDtypeStruct((t, D_MODEL), BF16),
            jax.ShapeDtypeStruct((t, D_MODEL), BF16),
        ),
        grid=(t // tm,),
        in_specs=[
            pl.BlockSpec((tm, D_MODEL), row),
            pl.BlockSpec((tm, D_MODEL), row),
            _resident(wo.shape),
            pl.BlockSpec((1, D_MODEL), lambda i: (0, 0)),
            _resident(wq.shape),
        ],
        out_specs=tuple(pl.BlockSpec((tm, D_MODEL), row) for _ in range(3)),
        compiler_params=_params(("parallel",), 48),
        name="out_proj",
    )(x, m, wo, g.reshape(1, D_MODEL), wq)


def _top_rows(x, n):
    rows = x.shape[0]
    ridx = lax.broadcasted_iota(jnp.int32, x.shape, 0)
    out = []
    for _ in range(n):
        m = jnp.max(x, axis=0, keepdims=True)
        out.append(m)
        first = jnp.min(jnp.where(x == m, ridx, rows), axis=0, keepdims=True)
        x = jnp.where(ridx == first, -jnp.inf, x)
    return out


_TOP_N = PEER_TOPK + 1
_TOP_ROWS = 24
_CAND_ROWS = _TOP_ROWS + 7 * 8 + (_TOP_ROWS - 8)


def _peer_route_kernel(qp_ref, keys_ref, s1_ref, s2_ref, thr_ref, top_scr, cand_scr):
    k = PEER_TOPK
    top_scr[:, _TOP_N:, :] = jnp.full((2, _TOP_ROWS - _TOP_N, top_scr.shape[2]), -jnp.inf, F32)
    for h in range(PEER_HEADS):
        for c in range(2):
            hc = 2 * h + c
            st = _dot_nt(keys_ref[hc], qp_ref[:, hc * PEER_HALF:(hc + 1) * PEER_HALF])
            (s2_ref if c else s1_ref)[h] = st
            for r, m in enumerate(_top_rows(st, _TOP_N)):
                top_scr[c, r:r + 1, :] = m
        v1 = top_scr[0]
        v2 = top_scr[1]
        cand_scr[0:_TOP_ROWS, :] = v1[0:1] + v2
        for a in range(1, 8):
            cand_scr[_TOP_ROWS + 8 * (a - 1):_TOP_ROWS + 8 * a, :] = v1[a:a + 1] + v2[0:8]
        cand_scr[_TOP_ROWS + 56:, :] = v1[8:_TOP_ROWS] + v2[0:1]
        best = _top_rows(cand_scr[...], _TOP_N)
        z = jnp.zeros_like(best[0])
        for m in best[:k]:
            z = z + jnp.exp(m - best[0])
        off = best[0] + jnp.log(z)
        thr_ref[h:h + 1, :] = (0.5 * (best[k - 1] + best[k]) - off) * LOG2E - 1.0
        s1_ref[h] = (s1_ref[h] - off) * LOG2E - 1.0
        s2_ref[h] = s2_ref[h] * LOG2E


def _peer_route(qp, keys):
    t = qp.shape[0]
    tm = _pick(t, 512)
    return pl.pallas_call(
        _peer_route_kernel,
        out_shape=(
            jax.ShapeDtypeStruct((PEER_HEADS, N_KEYS, t), F32),
            jax.ShapeDtypeStruct((PEER_HEADS, N_KEYS, t), F32),
            jax.ShapeDtypeStruct((PEER_HEADS, t), F32),
        ),
        grid=(t // tm,),
        in_specs=[
            pl.BlockSpec((tm, D_MODEL), lambda i: (i, 0)),
            pl.BlockSpec(keys.shape, lambda i: (0, 0, 0)),
        ],
        out_specs=(
            pl.BlockSpec((PEER_HEADS, N_KEYS, tm), lambda i: (0, 0, i)),
            pl.BlockSpec((PEER_HEADS, N_KEYS, tm), lambda i: (0, 0, i)),
            pl.BlockSpec((PEER_HEADS, tm), lambda i: (0, i)),
        ),
        scratch_shapes=[pltpu.VMEM((2, _TOP_ROWS, tm), F32), pltpu.VMEM((_CAND_ROWS, tm), F32)],
        compiler_params=_params(("parallel",), 40),
        name="peer_route",
    )(qp, keys)


GATE_ROWS = 16


def _ordered_zero(x):
    bits = lax.bitcast_convert_type(x, jnp.int32)
    return lax.shift_right_logical(lax.shift_right_logical(bits, 16), 16).astype(F32)


def _peer_expert_kernel(h2_ref, u_ref, v_ref, s1_first_ref, s1_next_ref, s2_ref, thr_ref, x1_ref, gf_ref,
                        o_ref, w_scr, act_scr, *, tm, te, n_tiles):
    j = pl.program_id(1)
    n_key1 = te // N_KEYS

    def build_gates(s1_ref, slot, key1_rows):
        link = jnp.zeros((8, LANES), F32)
        for a in key1_rows:
            for g in range(tm // LANES):
                ls = slice(g * LANES, (g + 1) * LANES)
                for r in range(N_KEYS // GATE_ROWS):
                    k2 = slice(r * GATE_ROWS, (r + 1) * GATE_ROWS)
                    w = jnp.concatenate([link] * (GATE_ROWS // 8), axis=0)
                    for h in range(PEER_HEADS):
                        t = s1_ref[h, a:a + 1, ls] + s2_ref[h, k2, ls]
                        w = w + jnp.where(t >= thr_ref[h:h + 1, ls], jnp.exp2(t), 0.0)
                    w_scr[slot, a * N_KEYS + r * GATE_ROWS:a * N_KEYS + (r + 1) * GATE_ROWS, ls] = w
                    link = _ordered_zero(w[0:8])
        return link

    def run_experts(slot, tie):
        sc = _dot_nt(u_ref[...], h2_ref[...])
        for a in range(n_key1):
            for g in range(tm // LANES):
                ls = slice(g * LANES, (g + 1) * LANES)
                rows = slice(a * N_KEYS, (a + 1) * N_KEYS)
                x = sc[rows, ls]
                act = (x * (1.0 + lax.erf(x * math.sqrt(0.5)))) * w_scr[slot, rows, ls]
                if tie is not None and a == n_key1 - 1 and g == tm // LANES - 1:
                    act = jnp.concatenate([act[0:8] + tie, act[8:]], axis=0)
                act_scr[rows, ls] = act.astype(BF16)
        o_ref[...] += _dot_tn(act_scr[...], v_ref[...])

    @pl.when(j == 0)
    def _():
        o_ref[...] = jnp.zeros_like(o_ref)
        build_gates(s1_first_ref, 0, range(n_key1))

    for parity in range(2):
        @pl.when((lax.rem(j, 2) == parity) & (j < n_tiles - 1))
        def _():
            tie = build_gates(s1_next_ref, 1 - parity, range(0, n_key1 // 2))
            run_experts(parity, tie)
            build_gates(s1_next_ref, 1 - parity, range(n_key1 // 2, n_key1))

        if (n_tiles - 1) % 2 == parity:
            @pl.when(j == n_tiles - 1)
            def _():
                run_experts(parity, None)
                x = x1_ref[...] + o_ref[...]
                r = lax.rsqrt(jnp.mean(x * x, axis=-1, keepdims=True) + EPS)
                o_ref[...] = (x * r) * gf_ref[...]


def _peer_experts(h2, u, v, s1, s2, thr, x1, g_final):
    t = h2.shape[0]
    ne = u.shape[0]
    tm = _pick(t, 512)
    te = 1024
    n_tiles = ne // te
    n_key1 = te // N_KEYS
    assert n_tiles >= 2 and n_key1 % 8 == 0
    kern = functools.partial(_peer_expert_kernel, tm=tm, te=te, n_tiles=n_tiles)
    return pl.pallas_call(
        kern,
        out_shape=jax.ShapeDtypeStruct((t, D_MODEL), F32),
        grid=(t // tm, n_tiles),
        in_specs=[
            pl.BlockSpec((tm, D_MODEL), lambda i, j: (i, 0)),
            pl.BlockSpec((te, D_MODEL), lambda i, j: (j, 0)),
            pl.BlockSpec((te, D_MODEL), lambda i, j: (j, 0)),
            pl.BlockSpec((PEER_HEADS, n_key1, tm), lambda i, j: (0, 0, i)),
            pl.BlockSpec((PEER_HEADS, n_key1, tm), lambda i, j: (0, jnp.minimum(j + 1, n_tiles - 1), i)),
            pl.BlockSpec((PEER_HEADS, N_KEYS, tm), lambda i, j: (0, 0, i)),
            pl.BlockSpec((PEER_HEADS, tm), lambda i, j: (0, i)),
            pl.BlockSpec((tm, D_MODEL), lambda i, j: (i, 0)),
            pl.BlockSpec((1, D_MODEL), lambda i, j: (0, 0)),
        ],
        out_specs=pl.BlockSpec((tm, D_MODEL), lambda i, j: (i, 0)),
        scratch_shapes=[pltpu.VMEM((2, te, tm), F32), pltpu.VMEM((te, tm), BF16)],
        compiler_params=_params(("parallel", "arbitrary"), 58),
        name="peer_experts",
    )(h2, u, v, s1, s1, s2, thr, x1, g_final.reshape(1, D_MODEL))


def _stream(x, w, past):
    b, l, d = x.shape
    t = b * l
    x2 = x.reshape(t, d)
    h, qa, kf, vf, kb, vb = _proj_attn(x2, w["norm_mix"], w["w_attn"])
    if past is None:
        qr, kr, vr = _proj_ret(h, w["w_ret"], l, 0)
    else:
        qr, kr, vr = _proj_ret(h, w["w_ret"], l, past[3])
    as3 = lambda a: a.reshape(b, l, a.shape[-1])
    if past is None:
        oa = _attn_prompt(as3(qa), as3(kb), as3(vb), w["lam"], w["subln"])
        blk = _pick(l, 512)
        orn, state = _retention(as3(qr), as3(kr), as3(vr), w["ret_gn"], None, blk)
    else:
        cache_k, cache_v, state0, _ = past
        oa = _attn_sample(as3(qa), as3(kb), as3(vb), cache_k, cache_v, w["lam"], w["subln"])
        orn, state = _retention(as3(qr), as3(kr), as3(vr), w["ret_gn"], state0, l)
    orp = _swish_gate(h, w["w_gs"], orn.reshape(t, R_V))
    m = _merge(h, oa.reshape(t, A_V), orp, w["w_ga"], w["w_gr"], w["w_a"], w["w_r"])
    x1, h2, qp = _out_proj(x2, m, w["w_o"], w["norm_ffn"], w["w_q"])
    s1, s2, thr = _peer_route(qp, w["keys"])
    y = _peer_experts(h2, w["u"], w["v"], s1, s2, thr, x1, w["norm_final"])
    k_rows = kf.reshape(1, b, l, A_HEADS, 2 * A_DIM)
    v_rows = vf.reshape(1, b, l, A_HEADS, A_VDIM)
    return y.reshape(b, l, d), k_rows, v_rows, state[None]


def kernel(x_prompt, x_sample, cache_k, cache_v, state_ret, norm_mix, w_in, lambda_q1, lambda_k1, lambda_q2, lambda_k2, attn_subln, ret_gn, w_branch_a, w_branch_r, w_out, norm_ffn, peer_wq, peer_keys, peer_u, peer_v, norm_final):
    assert w_in.shape[0] == 1, "single layer"
    wi = w_in[0]
    o_ret = 2 * A_QK + A_V
    o_gs = o_ret + 2 * R_QK + R_V
    o_ga = o_gs + R_V
    o_gr = o_ga + D_MODEL
    w = {
        "norm_mix": norm_mix[0],
        "w_attn": wi[:, :o_ret].astype(BF16),
        "w_ret": wi[:, o_ret:o_gs].astype(BF16),
        "w_gs": wi[:, o_gs:o_ga].astype(BF16),
        "w_ga": wi[:, o_ga:o_gr].astype(BF16),
        "w_gr": wi[:, o_gr:].astype(BF16),
        "lam": jnp.stack([lambda_q1[0], lambda_k1[0], lambda_q2[0], lambda_k2[0]]).astype(F32),
        "subln": attn_subln[0].reshape(1, A_VDIM),
        "ret_gn": ret_gn[0],
        "w_a": w_branch_a[0].astype(BF16),
        "w_r": w_branch_r[0].astype(BF16),
        "w_o": w_out[0].astype(BF16),
        "norm_ffn": norm_ffn[0],
        "w_q": peer_wq[0].astype(BF16),
        "keys": peer_keys[0].reshape(2 * PEER_HEADS, N_KEYS, PEER_HALF).astype(BF16),
        "u": peer_u.reshape(peer_u.shape[1:]).astype(BF16),
        "v": peer_v.reshape(peer_v.shape[1:]).astype(BF16),
        "norm_final": norm_final,
    }
    bd, past_len = cache_k.shape[1], cache_k.shape[2]
    past = (
        cache_k[0].reshape(bd, past_len, A_QK),
        cache_v[0].reshape(bd, past_len, A_V),
        state_ret[0],
        past_len,
    )
    y_p, k_p, v_p, s_p = _stream(x_prompt, w, None)
    y_s, k_s, v_s, s_s = _stream(x_sample, w, past)
    return (y_p, y_s, k_p, v_p, s_p, k_s, v_s, s_s)
```
